```python
import math
import jax, jax.numpy as jnp
from jax import lax
import numpy as np

D_MODEL = 1024
BATCH = 8
SEQ = 2048
DEPTH = 2
DEC_BATCH = 4
DEC_SEQ = 8192
PAST_LEN = 128

N_META = 16
MLA_HEADS = 16
QK_NOPE = 64
QK_ROPE = 32
V_HEAD = 64
Q_LORA = 384
KV_LORA = 256
ATT_W = MLA_HEADS * V_HEAD
ROPE_THETA = 10000.0
Q_BLOCK = 128
SSM_EXPAND = 2
D_INNER = SSM_EXPAND * D_MODEL
SSM_HEADDIM = 64
SSM_HEADS = D_INNER // SSM_HEADDIM
SSM_GROUPS = 4
D_STATE = 128
D_CONV = 5
CONV_CH = D_INNER + 2 * SSM_GROUPS * D_STATE
CHUNK = 128
D_FF = 2816
IN_SPLITS = (Q_LORA, KV_LORA, QK_ROPE, D_INNER, CONV_CH, 2 * SSM_HEADS, 2 * D_MODEL)
IN_W = sum(IN_SPLITS)
EPS = 1e-6

kernel_name = 'hybrid_mla_ssd_macaron_encoder'


def rmsnorm(x, g):
    x32 = x.astype(jnp.float32)
    y = x32 * lax.rsqrt(jnp.mean(x32 * x32, axis=-1, keepdims=True) + EPS)
    return (y * g.astype(jnp.float32)).astype(x.dtype)


def swiglu(x, w_in, w_out):
    gu = x @ w_in
    g, u = jnp.split(gu, 2, axis=-1)
    return (jax.nn.silu(g) * u) @ w_out


def split_cols(x, sizes):
    offs = np.cumsum(sizes)[:-1].tolist()
    return jnp.split(x, offs, axis=-1)


def rope_tables(length, dim):
    pos = jnp.arange(length, dtype=jnp.float32)
    inv = ROPE_THETA ** (-jnp.arange(0, dim, 2, dtype=jnp.float32) / dim)
    ang = pos[:, None] * inv[None, :]
    ang = jnp.concatenate([ang, ang], axis=-1)
    return jnp.cos(ang), jnp.sin(ang)


def apply_rope(x, cos, sin):
    x32 = x.astype(jnp.float32)
    x1, x2 = jnp.split(x32, 2, axis=-1)
    rot = jnp.concatenate([-x2, x1], axis=-1)
    return (x32 * cos + rot * sin).astype(x.dtype)


def block_attention(q, k, v):
    b_, L, h, dq = q.shape
    nb = -(-L // Q_BLOCK)
    lp = nb * Q_BLOCK
    qp = jnp.pad(q, ((0, 0), (0, lp - L), (0, 0), (0, 0)))
    qb = jnp.moveaxis(qp.reshape(b_, nb, Q_BLOCK, h, dq), 1, 0)
    scale = dq ** -0.5

    def one(qblk):
        s = jnp.einsum('bqhd,bkhd->bhqk', qblk, k).astype(jnp.float32) * scale
        p = jax.nn.softmax(s, axis=-1).astype(v.dtype)
        return jnp.einsum('bhqk,bkhd->bqhd', p, v)

    o = lax.map(one, qb)
    return jnp.moveaxis(o, 0, 1).reshape(b_, lp, h, v.shape[-1])[:, :L]


def ssd_chunked(x, dt, a, bm, cm):
    b_, T, h, p = x.shape
    g, n = bm.shape[2], bm.shape[3]
    j = h // g
    c = T // CHUNK
    xr = (x.astype(jnp.float32) * dt[..., None]).reshape(b_, c, CHUNK, g, j, p)
    da = (dt * a.astype(jnp.float32)).reshape(b_, c, CHUNK, g, j)
    a_cs = jnp.cumsum(da, axis=2)
    bc = bm.astype(jnp.float32).reshape(b_, c, CHUNK, g, n)
    cc = cm.astype(jnp.float32).reshape(b_, c, CHUNK, g, n)
    idx = jnp.arange(CHUNK)
    lower = (idx[:, None] >= idx[None, :])[None, None, :, :, None, None]
    seg = a_cs[:, :, :, None] - a_cs[:, :, None, :]
    lmat = jnp.exp(jnp.where(lower, seg, -jnp.inf))
    cb = jnp.einsum('bclgn,bcsgn->bclsg', cc, bc)
    y_diag = jnp.einsum('bclsg,bclsgj,bcsgjp->bclgjp', cb, lmat, xr)
    decay_states = jnp.exp(a_cs[:, :, -1:] - a_cs)
    states = jnp.einsum('bcsgn,bcsgj,bcsgjp->bcgjpn', bc, decay_states, xr)
    chunk_decay = jnp.exp(a_cs[:, :, -1])

    def step(s, inp):
        st, dec = inp
        return s * dec[..., None, None] + st, s

    init = jnp.zeros((b_, g, j, p, n), jnp.float32)
    _, prev = lax.scan(step, init, (jnp.moveaxis(states, 1, 0), jnp.moveaxis(chunk_decay, 1, 0)))
    prev = jnp.moveaxis(prev, 0, 1)
    y_off = jnp.einsum('bclgn,bcgjpn,bclgj->bclgjp', cc, prev, jnp.exp(a_cs))
    return (y_diag + y_off).reshape(b_, T, h, p).astype(x.dtype)


def mamba_branch(z, xbc, dt_raw, conv_w, conv_b, a_log, dt_bias, d_skip, ssm_norm):
    b_, L, _ = xbc.shape
    xbc = lax.conv_general_dilated(xbc, conv_w[:, None, :].astype(xbc.dtype), window_strides=(1,),
                                   padding=[(D_CONV // 2, D_CONV // 2)],
                                   dimension_numbers=('NWC', 'WIO', 'NWC'),
                                   feature_group_count=CONV_CH)
    xbc = jax.nn.silu(xbc + conv_b)
    xs, bm, cm = split_cols(xbc, (D_INNER, SSM_GROUPS * D_STATE, SSM_GROUPS * D_STATE))
    xs = xs.reshape(b_, L, SSM_HEADS, SSM_HEADDIM)
    bm = bm.reshape(b_, L, SSM_GROUPS, D_STATE)
    cm = cm.reshape(b_, L, SSM_GROUPS, D_STATE)
    dt = jax.nn.softplus(dt_raw.astype(jnp.float32).reshape(b_, L, 2, SSM_HEADS)
                         + dt_bias.astype(jnp.float32))
    a = -jnp.exp(a_log.astype(jnp.float32))
    pad = (-L) % CHUNK
    padt = lambda t: jnp.pad(t, ((0, 0), (pad, 0)) + ((0, 0),) * (t.ndim - 2))
    xs_p, bm_p, cm_p, dt_p = padt(xs), padt(bm), padt(cm), padt(dt)
    y_f = ssd_chunked(xs_p, dt_p[:, :, 0], a[0], bm_p, cm_p)
    fl = lambda t: jnp.flip(t, axis=1)
    y_b = fl(ssd_chunked(fl(xs_p), fl(dt_p[:, :, 1]), a[1], fl(bm_p), fl(cm_p)))
    y = (y_f + y_b)[:, pad:] + xs * d_skip[:, None].astype(xs.dtype)
    y = y.reshape(b_, L, D_INNER) * jax.nn.silu(z)
    yg = y.reshape(b_, L, SSM_GROUPS, D_INNER // SSM_GROUPS).astype(jnp.float32)
    yg = yg * lax.rsqrt(jnp.mean(yg * yg, axis=-1, keepdims=True) + EPS)
    return (yg.reshape(b_, L, D_INNER) * ssm_norm.astype(jnp.float32)).astype(z.dtype)


def hybrid_mixer(h, w_in, q_norm, w_uq, kv_norm, w_ukv, conv_w, conv_b, a_log, dt_bias, d_skip,
                 ssm_norm, w_branch, w_out):
    b_, L, _ = h.shape
    proj = h @ w_in
    cq, ckv, kr, z, xbc, dt_raw, gate_logits = split_cols(proj, IN_SPLITS)
    cos, sin = rope_tables(L, QK_ROPE)
    q = (rmsnorm(cq, q_norm) @ w_uq).reshape(b_, L, MLA_HEADS, QK_NOPE + QK_ROPE)
    q_nope, q_rope = q[..., :QK_NOPE], q[..., QK_NOPE:]
    q_rope = apply_rope(q_rope, cos[None, :, None, :], sin[None, :, None, :])
    kv = (rmsnorm(ckv, kv_norm) @ w_ukv).reshape(b_, L, MLA_HEADS, QK_NOPE + V_HEAD)
    k_nope, v = kv[..., :QK_NOPE], kv[..., QK_NOPE:]
    k_rope = apply_rope(kr, cos[None], sin[None])
    k_rope = jnp.broadcast_to(k_rope[:, :, None, :], (b_, L, MLA_HEADS, QK_ROPE))
    qf = jnp.concatenate([q_nope, q_rope], axis=-1)
    kf = jnp.concatenate([k_nope, k_rope], axis=-1)
    o_a = block_attention(qf, kf, v).reshape(b_, L, ATT_W)
    o_m = mamba_branch(z, xbc, dt_raw, conv_w, conv_b, a_log, dt_bias, d_skip, ssm_norm)
    y_a = o_a @ w_branch[:ATT_W]
    y_m = o_m @ w_branch[ATT_W:]
    gates = jax.nn.sigmoid(gate_logits.astype(jnp.float32)).reshape(b_, L, 2, D_MODEL).astype(h.dtype)
    mix = gates[:, :, 0] * y_a + gates[:, :, 1] * y_m
    return mix @ w_out


def trunk(x, meta_tokens, ffn1_norm, ffn1_w_in, ffn1_w_out, mix_norm, w_in, q_norm, w_uq, kv_norm,
          w_ukv, conv_w, conv_b, a_log, dt_bias, d_skip, ssm_norm, w_branch, w_out, ffn2_norm,
          ffn2_w_in, ffn2_w_out, final_norm):
    b_ = x.shape[0]
    meta = jnp.broadcast_to(meta_tokens[None].astype(x.dtype), (b_, N_META, D_MODEL))
    x = jnp.concatenate([meta, x], axis=1)
    for l in range(DEPTH):
        x = x + 0.5 * swiglu(rmsnorm(x, ffn1_norm[l]), ffn1_w_in[l], ffn1_w_out[l])
        h = rmsnorm(x, mix_norm[l])
        x = x + hybrid_mixer(h, w_in[l], q_norm[l], w_uq[l], kv_norm[l], w_ukv[l], conv_w[l], conv_b[l],
                             a_log[l], dt_bias[l], d_skip[l], ssm_norm[l], w_branch[l], w_out[l])
        x = x + 0.5 * swiglu(rmsnorm(x, ffn2_norm[l]), ffn2_w_in[l], ffn2_w_out[l])
    x = rmsnorm(x, final_norm)
    return x[:, N_META:]


def setup_inputs(seed: int = 0) -> dict:
    key = jax.random.key(seed)
    ks = jax.random.split(key, 32)
    nrm = lambda k, shape, s: jax.random.normal(k, shape, jnp.float32) * s
    gain = lambda k, shape: 1.0 + 0.02 * jax.random.normal(k, shape, jnp.float32)
    dt0 = jnp.exp(jax.random.uniform(ks[20], (DEPTH, 2, SSM_HEADS), jnp.float32,
                                     math.log(1e-3), math.log(1e-1)))
    w_branch = jnp.concatenate([nrm(ks[21], (DEPTH, ATT_W, D_MODEL), ATT_W ** -0.5),
                                nrm(ks[22], (DEPTH, D_INNER, D_MODEL), D_INNER ** -0.5)], axis=1)
    return {
        'x_prompt': nrm(ks[0], (BATCH, SEQ, D_MODEL), 1.0),
        'x_sample': nrm(ks[1], (DEC_BATCH, DEC_SEQ, D_MODEL), 1.0),
        'meta_tokens': nrm(ks[2], (N_META, D_MODEL), 1.0),
        'ffn1_norm': gain(ks[3], (DEPTH, D_MODEL)),
        'ffn1_w_in': nrm(ks[4], (DEPTH, D_MODEL, 2 * D_FF), D_MODEL ** -0.5),
        'ffn1_w_out': nrm(ks[5], (DEPTH, D_FF, D_MODEL), D_FF ** -0.5),
        'mix_norm': gain(ks[6], (DEPTH, D_MODEL)),
        'w_in': nrm(ks[7], (DEPTH, D_MODEL, IN_W), D_MODEL ** -0.5),
        'q_norm': gain(ks[8], (DEPTH, Q_LORA)),
        'w_uq': nrm(ks[9], (DEPTH, Q_LORA, MLA_HEADS * (QK_NOPE + QK_ROPE)), Q_LORA ** -0.5),
        'kv_norm': gain(ks[10], (DEPTH, KV_LORA)),
        'w_ukv': nrm(ks[11], (DEPTH, KV_LORA, MLA_HEADS * (QK_NOPE + V_HEAD)), KV_LORA ** -0.5),
        'conv_w': nrm(ks[12], (DEPTH, D_CONV, CONV_CH), D_CONV ** -0.5),
        'conv_b': nrm(ks[13], (DEPTH, CONV_CH), 0.02),
        'a_log': jnp.log(jax.random.uniform(ks[14], (DEPTH, 2, SSM_HEADS), jnp.float32, 1.0, 16.0)),
        'dt_bias': dt0 + jnp.log(-jnp.expm1(-dt0)),
        'd_skip': gain(ks[15], (DEPTH, SSM_HEADS)),
        'ssm_norm': gain(ks[16], (DEPTH, D_INNER)),
        'w_branch': w_branch,
        'w_out': nrm(ks[17], (DEPTH, D_MODEL, D_MODEL), D_MODEL ** -0.5),
        'ffn2_norm': gain(ks[18], (DEPTH, D_MODEL)),
        'ffn2_w_in': nrm(ks[19], (DEPTH, D_MODEL, 2 * D_FF), D_MODEL ** -0.5),
        'ffn2_w_out': nrm(ks[23], (DEPTH, D_FF, D_MODEL), D_FF ** -0.5),
        'final_norm': gain(ks[24], (D_MODEL,)),
    }


def reference(x_prompt, x_sample, meta_tokens, ffn1_norm, ffn1_w_in, ffn1_w_out, mix_norm, w_in, q_norm,
              w_uq, kv_norm, w_ukv, conv_w, conv_b, a_log, dt_bias, d_skip, ssm_norm, w_branch, w_out,
              ffn2_norm, ffn2_w_in, ffn2_w_out, final_norm):
    y_prompt = trunk(x_prompt, meta_tokens, ffn1_norm, ffn1_w_in, ffn1_w_out, mix_norm, w_in, q_norm, w_uq,
                     kv_norm, w_ukv, conv_w, conv_b, a_log, dt_bias, d_skip, ssm_norm, w_branch, w_out,
                     ffn2_norm, ffn2_w_in, ffn2_w_out, final_norm)
    y_sample = trunk(x_sample, meta_tokens, ffn1_norm, ffn1_w_in, ffn1_w_out, mix_norm, w_in, q_norm, w_uq,
                     kv_norm, w_ukv, conv_w, conv_b, a_log, dt_bias, d_skip, ssm_norm, w_branch, w_out,
                     ffn2_norm, ffn2_w_in, ffn2_w_out, final_norm)
    return (y_prompt, y_sample)
```

```python
import functools

import numpy as np
import jax
import jax.numpy as jnp
from jax import lax
from jax.experimental import pallas as pl
from jax.experimental.pallas import tpu as pltpu

D_MODEL = 1024
N_META = 16
MLA_HEADS = 16
QK_NOPE = 64
QK_ROPE = 32
V_HEAD = 64
Q_LORA = 384
KV_LORA = 256
ATT_W = MLA_HEADS * V_HEAD
ROPE_THETA = 10000.0
D_INNER = 2048
SSM_HEADDIM = 64
SSM_HEADS = 32
SSM_GROUPS = 4
D_STATE = 128
D_CONV = 5
CONV_CH = D_INNER + 2 * SSM_GROUPS * D_STATE
CHUNK = 128
D_FF = 2816
EPS = 1e-6

LANES = 128
HALO = 8
PAD_ROWS = CHUNK - N_META
HEAD_PAIRS = MLA_HEADS // 2
QK_PAD = LANES
TM = 512
TM_SSM = 256
TQ = 512
TK = 512
FF_CHUNK = 256
NEG = -1e30
VMEM_LIMIT = 56 * 1024 * 1024

F32 = jnp.float32
BF16 = jnp.bfloat16


def _cparams(n_axes):
    return pltpu.CompilerParams(dimension_semantics=("arbitrary",) * n_axes,
                                vmem_limit_bytes=VMEM_LIMIT)


def _resident(shape):
    nd = len(shape)
    return pl.BlockSpec(shape, lambda *_: (0,) * nd, pipeline_mode=pl.Buffered(1))


def _rms(x, g):
    ms = jnp.mean(x * x, axis=-1, keepdims=True)
    return x * lax.rsqrt(ms + EPS) * g


def _sigmoid(x):
    return 1.0 / (1.0 + jnp.exp(-x))


def _dot(a, b):
    return jnp.dot(a, b, preferred_element_type=F32)


def _dot_nt(a, b):
    return lax.dot_general(a, b, (((1,), (1,)), ((), ())), preferred_element_type=F32)


def _ffn_kernel(x_ref, g_ref, win_ref, wout_ref, *rest, final):
    if final:
        fg_ref, o_ref, acc_ref = rest
    else:
        o_ref, acc_ref = rest
    x = x_ref[...]
    h = _rms(x, g_ref[...]).astype(BF16)
    for c in range(D_FF // FF_CHUNK):
        lo = c * FF_CHUNK
        g = _dot(h, win_ref[:, lo:lo + FF_CHUNK])
        u = _dot(h, win_ref[:, D_FF + lo:D_FF + lo + FF_CHUNK])
        a = (g * _sigmoid(g) * u).astype(BF16)
        part = _dot(a, wout_ref[lo:lo + FF_CHUNK, :])
        if c == 0:
            acc_ref[...] = part
        else:
            acc_ref[...] += part
    y = x + 0.5 * acc_ref[...]
    if final:
        y = _rms(y, fg_ref[...])
    o_ref[...] = y


def _ffn(x, gain, w_in, w_out, final_gain=None):
    n = x.shape[0]
    final = final_gain is not None
    row = pl.BlockSpec((TM, D_MODEL), lambda i: (i, 0))
    in_specs = [row, _resident((1, D_MODEL)), _resident((D_MODEL, 2 * D_FF)), _resident((D_FF, D_MODEL))]
    args = [x, gain.reshape(1, D_MODEL), w_in, w_out]
    if final:
        in_specs.append(_resident((1, D_MODEL)))
        args.append(final_gain.reshape(1, D_MODEL))
    return pl.pallas_call(
        functools.partial(_ffn_kernel, final=final),
        grid=(n // TM,),
        in_specs=in_specs,
        out_specs=row,
        out_shape=jax.ShapeDtypeStruct((n, D_MODEL), F32),
        scratch_shapes=[pltpu.VMEM((TM, D_MODEL), F32)],
        compiler_params=_cparams(1),
        name="ffn",
    )(*args)


S_COLS = Q_LORA + KV_LORA + 2 * LANES


def _qkv_kernel(x_ref, g_ref, ws_ref, qn_ref, wqa_ref, wqb_ref, kvn_ref, wk_ref, wv_ref, cos_ref, sin_ref,
                q_ref, k_ref, v_ref):
    h = _rms(x_ref[...], g_ref[...]).astype(BF16)
    ps = _dot(h, ws_ref[...])
    cqn = _rms(ps[:, :Q_LORA], qn_ref[...]).astype(BF16)
    ckvn = _rms(ps[:, Q_LORA:Q_LORA + KV_LORA], kvn_ref[...]).astype(BF16)
    kr = ps[:, Q_LORA + KV_LORA:Q_LORA + KV_LORA + LANES]
    krr = ps[:, Q_LORA + KV_LORA + LANES:]
    cos = cos_ref[...]
    sin = sin_ref[...]
    scale = float((QK_NOPE + QK_ROPE) ** -0.5)
    lane = lax.broadcasted_iota(jnp.int32, (1, LANES), 1)
    cosq = jnp.where(lane < QK_NOPE, scale, cos * scale)
    sinq = sin * scale
    krope = kr * cos + krr * sin
    qa = _dot(cqn, wqa_ref[...])
    qb = _dot(cqn, wqb_ref[...])
    ka = _dot(ckvn, wk_ref[...])
    for hh in range(MLA_HEADS):
        sl = slice(hh * QK_PAD, (hh + 1) * QK_PAD)
        q_ref[hh] = (qa[:, sl] * cosq + qb[:, sl] * sinq).astype(BF16)
        k_ref[hh] = (ka[:, sl] + krope).astype(BF16)
    v_ref[...] = _dot(ckvn, wv_ref[...]).astype(BF16)


def _qkv(x, gain, ws, qn, wqa, wqb, kvn, wk, wv, cos_t, sin_t):
    n = x.shape[0]
    hq = MLA_HEADS * QK_PAD
    row = lambda w: pl.BlockSpec((TM, w), lambda i: (i, 0))
    head = pl.BlockSpec((MLA_HEADS, TM, QK_PAD), lambda i: (0, i, 0))
    return pl.pallas_call(
        _qkv_kernel,
        grid=(n // TM,),
        in_specs=[row(D_MODEL), _resident((1, D_MODEL)), _resident((D_MODEL, S_COLS)),
                  _resident((1, Q_LORA)), _resident((Q_LORA, hq)), _resident((Q_LORA, hq)),
                  _resident((1, KV_LORA)), _resident((KV_LORA, hq)), _resident((KV_LORA, ATT_W)),
                  row(LANES), row(LANES)],
        out_specs=[head, head, row(ATT_W)],
        out_shape=[jax.ShapeDtypeStruct((MLA_HEADS, n, QK_PAD), BF16),
                   jax.ShapeDtypeStruct((MLA_HEADS, n, QK_PAD), BF16),
                   jax.ShapeDtypeStruct((n, ATT_W), BF16)],
        compiler_params=_cparams(1),
        name="qkv_proj",
    )(x, gain.reshape(1, D_MODEL), ws, qn.reshape(1, Q_LORA), wqa, wqb, kvn.reshape(1, KV_LORA), wk, wv,
      cos_t, sin_t)


DT_COLS = 2 * LANES


def _ssm_proj_kernel(x_ref, g_ref, wz_ref, wx_ref, wdt_ref, dtb_ref, wg_ref, z_ref, xbc_ref, dt_ref, gate_ref,
                     *, n_main):
    h = _rms(x_ref[...], g_ref[...]).astype(BF16)
    row = pl.program_id(0) * TM_SSM + lax.broadcasted_iota(jnp.int32, (TM_SSM, 1), 0)
    unused = jnp.logical_and(row >= n_main, ((row - n_main) % CHUNK) < PAD_ROWS)
    z_ref[...] = _dot(h, wz_ref[...])
    xbc_ref[...] = jnp.where(unused, 0.0, _dot(h, wx_ref[...]))
    dtr = _dot(h, wdt_ref[...]) + dtb_ref[...]
    dt = jnp.maximum(dtr, 0.0) + jnp.log(1.0 + jnp.exp(-jnp.abs(dtr)))
    lane = lax.broadcasted_iota(jnp.int32, (1, DT_COLS), 1)
    dead = jnp.logical_or(unused, (lane % LANES) >= SSM_HEADS)
    dt_ref[...] = jnp.where(dead, 0.0, dt)
    gate_ref[...] = _sigmoid(_dot(h, wg_ref[...]))


def _ssm_proj(x, gain, wz, wx, wdt, dtb, wg, n_main):
    n = x.shape[0]
    row = lambda w: pl.BlockSpec((TM_SSM, w), lambda i: (i, 0))
    return pl.pallas_call(
        functools.partial(_ssm_proj_kernel, n_main=n_main),
        grid=(n // TM_SSM,),
        in_specs=[row(D_MODEL), _resident((1, D_MODEL)), _resident((D_MODEL, D_INNER)),
                  _resident((D_MODEL, CONV_CH)), _resident((D_MODEL, DT_COLS)), _resident((1, DT_COLS)),
                  _resident((D_MODEL, 2 * D_MODEL))],
        out_specs=[row(D_INNER), row(CONV_CH), row(DT_COLS), row(2 * D_MODEL)],
        out_shape=[jax.ShapeDtypeStruct((n, D_INNER), F32), jax.ShapeDtypeStruct((n, CONV_CH), F32),
                   jax.ShapeDtypeStruct((n, DT_COLS), F32), jax.ShapeDtypeStruct((n, 2 * D_MODEL), F32)],
        compiler_params=_cparams(1),
        name="ssm_proj",
    )(x, gain.reshape(1, D_MODEL), wz, wx, wdt, dtb, wg)


def _attn_kernel(q_ref, km_ref, kmeta_ref, vm_ref, vmeta_ref, prev_ref, o_ref, *, seq, tq):
    del prev_ref
    lane = lax.broadcasted_iota(jnp.int32, (1, LANES), 1)
    outs = []
    for hh in range(2):
        q = q_ref[hh]
        s = _dot_nt(q, kmeta_ref[hh])
        s = jnp.where(lane >= PAD_ROWS, s, NEG)
        m = jnp.max(s, axis=-1, keepdims=True)
        p = jnp.exp(s - m)
        l = jnp.sum(p, axis=-1, keepdims=True)
        acc = _dot(p.astype(BF16), vmeta_ref[...])

        def body(j, carry, hh=hh, q=q):
            m, l, acc = carry
            off = pl.multiple_of(j * TK, TK)
            s = _dot_nt(q, km_ref[hh, pl.ds(off, TK), :])
            m_new = jnp.maximum(m, jnp.max(s, axis=-1, keepdims=True))
            alpha = jnp.exp(m - m_new)
            p = jnp.exp(s - m_new)
            l = alpha * l + jnp.sum(p, axis=-1, keepdims=True)
            acc = alpha * acc + _dot(p.astype(BF16), vm_ref[pl.ds(off, TK), :])
            return m_new, l, acc

        m, l, acc = lax.fori_loop(0, seq // TK, body, (m, l, acc))
        outs.append(acc / l)
    o_ref[...] = jnp.where(lane < V_HEAD, outs[0], outs[1]).astype(BF16)


def _attention(q, k, v, o_prev, *, n_main, main_off, seq_off, batch, seq, meta_queries):
    n = v.shape[0]
    meta_blk = n_main // CHUNK + seq_off
    kv_blk = main_off // seq
    if meta_queries:
        tq = CHUNK
        grid = (batch, HEAD_PAIRS, 1)
        q_row = lambda s, p, i: meta_blk + s
    else:
        tq = TQ
        grid = (batch, HEAD_PAIRS, seq // TQ)
        q_row = lambda s, p, i: main_off // TQ + s * (seq // TQ) + i
    in_specs = [
        pl.BlockSpec((2, tq, QK_PAD), lambda s, p, i: (p, q_row(s, p, i), 0)),
        pl.BlockSpec((2, seq, QK_PAD), lambda s, p, i: (p, kv_blk + s, 0)),
        pl.BlockSpec((2, CHUNK, QK_PAD), lambda s, p, i: (p, meta_blk + s, 0)),
        pl.BlockSpec((seq, LANES), lambda s, p, i: (kv_blk + s, p)),
        pl.BlockSpec((CHUNK, LANES), lambda s, p, i: (meta_blk + s, p)),
        pl.BlockSpec(memory_space=pl.ANY),
    ]
    return pl.pallas_call(
        functools.partial(_attn_kernel, seq=seq, tq=tq),
        grid=grid,
        in_specs=in_specs,
        out_specs=pl.BlockSpec((tq, LANES), lambda s, p, i: (q_row(s, p, i), p)),
        out_shape=jax.ShapeDtypeStruct((n, ATT_W), BF16),
        input_output_aliases={5: 0},
        compiler_params=_cparams(3),
        name="attn_meta" if meta_queries else "attn_main",
    )(q, k, k, v, v, o_prev)


def _ssd_kernel(*refs, n_chunks, backward, finalize):
    if finalize:
        (xbc_ref, hprev_ref, hnext_ref, dt_ref, cw_ref, cb_ref, alog_ref, z_ref, yprev_ref, norm_ref, alias_ref,
         o_ref, state_ref, xext_ref) = refs
    else:
        (xbc_ref, hprev_ref, hnext_ref, dt_ref, cw_ref, cb_ref, alog_ref, dskip_ref, alias_ref,
         o_ref, state_ref, xext_ref) = refs
    del alias_ref
    step = pl.program_id(1)
    c = (n_chunks - 1 - step) if backward else step

    @pl.when(step == 0)
    def _():
        state_ref[...] = jnp.zeros_like(state_ref)

    xext_ref[0:HALO, :] = jnp.where(c == 0, 0.0, hprev_ref[...])
    xext_ref[HALO:HALO + CHUNK, :] = xbc_ref[...]
    xext_ref[HALO + CHUNK:, :] = jnp.where(c == n_chunks - 1, 0.0, hnext_ref[...])
    conv = cb_ref[...]
    for kk in range(D_CONV):
        lo = HALO - D_CONV // 2 + kk
        conv = conv + cw_ref[kk:kk + 1, :] * xext_ref[lo:lo + CHUNK, :]
    xc = conv * _sigmoid(conv)
    xs = xc[:, :D_INNER]

    dt = dt_ref[...]
    a = -jnp.exp(alog_ref[...])
    da = dt * a
    hi = da.astype(BF16)
    r1 = da - hi.astype(F32)
    mid = r1.astype(BF16)
    lo3 = (r1 - mid.astype(F32)).astype(BF16)
    row_i = lax.broadcasted_iota(jnp.int32, (CHUNK, CHUNK), 0)
    col_i = lax.broadcasted_iota(jnp.int32, (CHUNK, CHUNK), 1)
    keep = (col_i >= row_i) if backward else (col_i <= row_i)
    tri = jnp.where(keep, 1.0, 0.0).astype(BF16)
    acs = _dot(tri, hi) + _dot(tri, mid) + _dot(tri, lo3)
    acs_t = acs.T
    dt_t = dt.T
    edge = 0 if backward else CHUNK - 1
    a_end_t = acs_t[:, edge:edge + 1]
    w_t = jnp.exp(a_end_t - acs_t) * dt_t
    lane = lax.broadcasted_iota(jnp.int32, (1, LANES), 1)
    first_half = lane < SSM_HEADDIM

    y_tiles = []
    for g in range(SSM_GROUPS):
        bg = xc[:, D_INNER + g * D_STATE:D_INNER + (g + 1) * D_STATE]
        cg = xc[:, D_INNER + SSM_GROUPS * D_STATE + g * D_STATE:D_INNER + SSM_GROUPS * D_STATE + (g + 1) * D_STATE]
        bg_t = bg.T
        cgb = cg.astype(BF16)
        cb = _dot_nt(cgb, bg.astype(BF16))
        gl = slice(g * 4 * LANES, (g + 1) * 4 * LANES)
        prev = state_ref[:, gl]
        y_off = _dot(cgb, prev.astype(BF16))
        new_tiles, decay_tiles = [], []
        for qq in range(4):
            h1 = g * 8 + qq * 2
            h2 = h1 + 1
            pair = slice(g * 4 * LANES + qq * LANES, g * 4 * LANES + (qq + 1) * LANES)
            x_pair = xs[:, pair]
            x_lo = jnp.where(first_half, x_pair, 0.0).astype(BF16)
            x_hi = jnp.where(first_half, 0.0, x_pair).astype(BF16)
            x_diag = jnp.concatenate([x_lo, x_hi], axis=0)
            ws, bs, cols = [], [], []
            for hh in (h1, h2):
                col = jnp.broadcast_to(acs[:, hh:hh + 1], (CHUNK, CHUNK))
                seg = col - acs_t[hh:hh + 1, :]
                lmat = jnp.exp(jnp.where(keep, seg, NEG))
                ws.append((cb * lmat * dt_t[hh:hh + 1, :]).astype(BF16))
                bs.append((bg_t * w_t[hh:hh + 1, :]).astype(BF16))
                cols.append(col)
            y_diag = _dot(jnp.concatenate(ws, axis=1), x_diag)
            new_tiles.append(_dot(jnp.concatenate(bs, axis=1), x_diag))
            acs_pair = jnp.where(first_half, cols[0], cols[1])
            decay_tiles.append(jnp.exp(acs_pair[edge:edge + 1, :]))
            y_tiles.append(y_diag + y_off[:, qq * LANES:(qq + 1) * LANES] * jnp.exp(acs_pair))
        state_ref[:, gl] = prev * jnp.concatenate(decay_tiles, axis=1) + jnp.concatenate(new_tiles, axis=1)
    y = jnp.concatenate(y_tiles, axis=1)

    if not finalize:
        o_ref[...] = y + xs * dskip_ref[...]
    else:
        z = z_ref[...]
        y = (y + yprev_ref[...]) * (z * _sigmoid(z))
        gw = D_INNER // SSM_GROUPS
        outs = []
        for g in range(SSM_GROUPS):
            yg = y[:, g * gw:(g + 1) * gw]
            outs.append(yg * lax.rsqrt(jnp.mean(yg * yg, axis=-1, keepdims=True) + EPS))
        o_ref[...] = (jnp.concatenate(outs, axis=1) * norm_ref[...]).astype(BF16)


def _ssd_pass(xbc, dt, cw, cb, alog, extra, o_prev, *, n_main, main_off, seq_off, batch, seq, backward, finalize):
    n = xbc.shape[0]
    n_chunks = seq // CHUNK + 1
    halo_per_chunk = CHUNK // HALO
    meta_blk = n_main // CHUNK + seq_off
    main_blk = main_off // CHUNK

    def chunk_of(step):
        return (n_chunks - 1 - step) if backward else step

    def blk(s, step):
        c = chunk_of(step)
        return jnp.where(c == 0, meta_blk + s, main_blk + s * (seq // CHUNK) + c - 1)

    def halo_prev(s, step):
        c = chunk_of(step)
        first_main = (meta_blk + s) * halo_per_chunk + halo_per_chunk - 1
        return jnp.where(c <= 1, first_main, (main_blk + s * (seq // CHUNK) + c - 1) * halo_per_chunk - 1)

    def halo_next(s, step):
        c = chunk_of(step)
        nxt = jnp.minimum(c, n_chunks - 2)
        return (main_blk + s * (seq // CHUNK) + nxt) * halo_per_chunk

    d = 1 if backward else 0
    chunk_rows = lambda w: pl.BlockSpec((CHUNK, w), lambda s, t: (blk(s, t), 0))
    in_specs = [
        chunk_rows(CONV_CH),
        pl.BlockSpec((HALO, CONV_CH), lambda s, t: (halo_prev(s, t), 0)),
        pl.BlockSpec((HALO, CONV_CH), lambda s, t: (halo_next(s, t), 0)),
        pl.BlockSpec((CHUNK, LANES), lambda s, t: (blk(s, t), d)),
        _resident((HALO, CONV_CH)), _resident((1, CONV_CH)), _resident((1, LANES)),
    ]
    args = [xbc, xbc, xbc, dt, cw, cb, alog]
    if finalize:
        z, y_fwd, norm = extra
        in_specs += [chunk_rows(D_INNER), chunk_rows(D_INNER), _resident((1, D_INNER))]
        args += [z, y_fwd, norm]
        out_dtype = BF16
    else:
        (dskip,) = extra
        in_specs += [_resident((1, D_INNER))]
        args += [dskip]
        out_dtype = F32
    in_specs.append(pl.BlockSpec(memory_space=pl.ANY))
    args.append(o_prev)
    return pl.pallas_call(
        functools.partial(_ssd_kernel, n_chunks=n_chunks, backward=backward, finalize=finalize),
        grid=(batch, n_chunks),
        in_specs=in_specs,
        out_specs=chunk_rows(D_INNER),
        out_shape=jax.ShapeDtypeStruct((n, D_INNER), out_dtype),
        scratch_shapes=[pltpu.VMEM((D_STATE, D_INNER), F32), pltpu.VMEM((CHUNK + 2 * HALO, CONV_CH), F32)],
        input_output_aliases={len(args) - 1: 0},
        compiler_params=_cparams(2),
        name="ssd_bwd" if backward else "ssd_fwd",
    )(*args)


def _merge_kernel(oa_ref, om_ref, gate_ref, x_ref, wba_ref, wbm_ref, wo_ref, o_ref):
    y_a = _dot(oa_ref[...], wba_ref[...])
    y_m = _dot(om_ref[...], wbm_ref[...])
    mix = gate_ref[:, :D_MODEL] * y_a + gate_ref[:, D_MODEL:] * y_m
    o_ref[...] = x_ref[...] + _dot(mix.astype(BF16), wo_ref[...])


def _merge(o_a, o_m, gates, x, wba, wbm, wo):
    n = x.shape[0]
    row = lambda w: pl.BlockSpec((TM, w), lambda i: (i, 0))
    return pl.pallas_call(
        _merge_kernel,
        grid=(n // TM,),
        in_specs=[row(ATT_W), row(D_INNER), row(2 * D_MODEL), row(D_MODEL),
                  _resident((ATT_W, D_MODEL)), _resident((D_INNER, D_MODEL)), _resident((D_MODEL, D_MODEL))],
        out_specs=row(D_MODEL),
        out_shape=jax.ShapeDtypeStruct((n, D_MODEL), F32),
        compiler_params=_cparams(1),
        name="merge_out",
    )(o_a, o_m, gates, x, wba, wbm, wo)


def _head_tiles(w, width):
    k = w.shape[0]
    w = w.reshape(k, MLA_HEADS, width)
    return jnp.pad(w, ((0, 0), (0, 0), (0, QK_PAD - width))).reshape(k, MLA_HEADS * QK_PAD)


def _rot_cols(w):
    half = QK_ROPE // 2
    return jnp.concatenate([-w[..., half:], w[..., :half]], axis=-1)


def _layer_weights(l, w_in, w_uq, w_ukv, conv_w, conv_b, a_log, dt_bias, d_skip, w_branch):
    offs = np.cumsum((Q_LORA, KV_LORA, QK_ROPE, D_INNER, CONV_CH, 2 * SSM_HEADS, 2 * D_MODEL))
    wi = w_in[l]
    w_cq, w_ckv, w_kr = wi[:, :offs[0]], wi[:, offs[0]:offs[1]], wi[:, offs[1]:offs[2]]
    w_z, w_xbc = wi[:, offs[2]:offs[3]], wi[:, offs[3]:offs[4]]
    w_dt, w_gate = wi[:, offs[4]:offs[5]], wi[:, offs[5]:offs[6]]
    rope_tile = lambda w: jnp.pad(w, ((0, 0), (QK_NOPE, QK_PAD - QK_NOPE - QK_ROPE)))
    ws = jnp.concatenate([w_cq, w_ckv, rope_tile(w_kr), rope_tile(_rot_cols(w_kr))], axis=1).astype(BF16)
    uq = w_uq[l].reshape(Q_LORA, MLA_HEADS, QK_NOPE + QK_ROPE)
    uq_rot = jnp.concatenate([jnp.zeros_like(uq[..., :QK_NOPE]), _rot_cols(uq[..., QK_NOPE:])], axis=-1)
    wqa = _head_tiles(uq.reshape(Q_LORA, -1), QK_NOPE + QK_ROPE).astype(BF16)
    wqb = _head_tiles(uq_rot.reshape(Q_LORA, -1), QK_NOPE + QK_ROPE).astype(BF16)
    ukv = w_ukv[l].reshape(KV_LORA, MLA_HEADS, QK_NOPE + V_HEAD)
    wk = _head_tiles(ukv[..., :QK_NOPE].reshape(KV_LORA, -1), QK_NOPE).astype(BF16)
    wv = ukv[..., QK_NOPE:].reshape(KV_LORA, ATT_W).astype(BF16)
    dt_tile = lambda w: jnp.pad(w, ((0, 0), (0, LANES - SSM_HEADS)))
    wdt = jnp.concatenate([dt_tile(w_dt[:, :SSM_HEADS]), dt_tile(w_dt[:, SSM_HEADS:])], axis=1).astype(BF16)
    dtb = jnp.concatenate([dt_tile(dt_bias[l][0:1]), dt_tile(dt_bias[l][1:2])], axis=1)
    return dict(
        ws=ws, wqa=wqa, wqb=wqb, wk=wk, wv=wv,
        wz=w_z.astype(BF16), wx=w_xbc.astype(BF16), wdt=wdt, dtb=dtb, wg=w_gate.astype(BF16),
        cw=jnp.pad(conv_w[l], ((0, HALO - D_CONV), (0, 0))), cb=conv_b[l].reshape(1, CONV_CH),
        alog=[dt_tile(a_log[l][0:1]), dt_tile(a_log[l][1:2])],
        dskip=jnp.repeat(d_skip[l], SSM_HEADDIM).reshape(1, D_INNER),
        wba=w_branch[l][:ATT_W].astype(BF16), wbm=w_branch[l][ATT_W:].astype(BF16),
    )


def _rope_tables(groups, n_main, n):
    inv = ROPE_THETA ** (-jnp.arange(0, QK_ROPE, 2, dtype=F32) / QK_ROPE)
    pos = [jnp.tile(jnp.arange(N_META, N_META + s, dtype=F32), b) for b, s in groups]
    meta = jnp.arange(CHUNK, dtype=F32) - PAD_ROWS
    pos.append(jnp.tile(meta, (n - n_main) // CHUNK))
    ang = jnp.concatenate(pos)[:, None] * inv[None, :]
    ang = jnp.concatenate([ang, ang], axis=-1)
    place = lambda t: jnp.pad(t, ((0, 0), (QK_NOPE, QK_PAD - QK_NOPE - QK_ROPE)))
    return place(jnp.cos(ang)), place(jnp.sin(ang))


def kernel(x_prompt, x_sample, meta_tokens, ffn1_norm, ffn1_w_in, ffn1_w_out, mix_norm, w_in, q_norm, w_uq, kv_norm, w_ukv, conv_w, conv_b, a_log, dt_bias, d_skip, ssm_norm, w_branch, w_out, ffn2_norm, ffn2_w_in, ffn2_w_out, final_norm):
    xs_in = (x_prompt, x_sample)
    groups = [(x.shape[0], x.shape[1]) for x in xs_in]
    depth = w_in.shape[0]
    n_seq = sum(b for b, _ in groups)
    n_main = sum(b * s for b, s in groups)
    n = n_main + -(-(n_seq * CHUNK) // TM) * TM
    main_offs, seq_offs = [], []
    mo = so = 0
    for b, s in groups:
        assert s % TQ == 0 and s % TK == 0 and mo % s == 0, (groups,)
        main_offs.append(mo)
        seq_offs.append(so)
        mo += b * s
        so += b

    meta_block = jnp.concatenate([jnp.zeros((PAD_ROWS, D_MODEL), F32), meta_tokens.astype(F32)], axis=0)
    x = jnp.concatenate([xi.reshape(-1, D_MODEL) for xi in xs_in]
                        + [jnp.tile(meta_block, ((n - n_main) // CHUNK, 1))], axis=0)
    cos_t, sin_t = _rope_tables(groups, n_main, n)

    for l in range(depth):
        lw = _layer_weights(l, w_in, w_uq, w_ukv, conv_w, conv_b, a_log, dt_bias, d_skip, w_branch)
        x = _ffn(x, ffn1_norm[l], ffn1_w_in[l].astype(BF16), ffn1_w_out[l].astype(BF16))

        q, k, v = _qkv(x, mix_norm[l], lw["ws"], q_norm[l], lw["wqa"], lw["wqb"], kv_norm[l], lw["wk"], lw["wv"],
                       cos_t, sin_t)
        z, xbc, dt, gates = _ssm_proj(x, mix_norm[l], lw["wz"], lw["wx"], lw["wdt"], lw["dtb"], lw["wg"], n_main)

        o_a = jnp.zeros((n, ATT_W), BF16)
        y_f = jnp.zeros((n, D_INNER), F32)
        o_m = jnp.zeros((n, D_INNER), BF16)
        geo = [dict(n_main=n_main, main_off=main_offs[g], seq_off=seq_offs[g], batch=groups[g][0], seq=groups[g][1])
               for g in range(len(groups))]
        for gg in geo:
            o_a = _attention(q, k, v, o_a, meta_queries=False, **gg)
            o_a = _attention(q, k, v, o_a, meta_queries=True, **gg)
            y_f = _ssd_pass(xbc, dt, lw["cw"], lw["cb"], lw["alog"][0], (lw["dskip"],), y_f,
                            backward=False, finalize=False, **gg)
        for gg in geo:
            o_m = _ssd_pass(xbc, dt, lw["cw"], lw["cb"], lw["alog"][1], (z, y_f, ssm_norm[l].reshape(1, D_INNER)),
                            o_m, backward=True, finalize=True, **gg)

        x = _merge(o_a, o_m, gates, x, lw["wba"], lw["wbm"], w_out[l].astype(BF16))
        x = _ffn(x, ffn2_norm[l], ffn2_w_in[l].astype(BF16), ffn2_w_out[l].astype(BF16),
                 final_gain=final_norm if l == depth - 1 else None)

    outs, off = [], 0
    for (b, s), xi in zip(groups, xs_in):
        outs.append(x[off:off + b * s].reshape(b, s, D_MODEL).astype(xi.dtype))
        off += b * s
    return tuple(outs)
```

```python
import functools

import numpy as np
import jax
import jax.numpy as jnp
from jax import lax
from jax.experimental import pallas as pl
from jax.experimental.pallas import tpu as pltpu

D_MODEL = 1024
N_META = 16
MLA_HEADS = 16
QK_NOPE = 64
QK_ROPE = 32
V_HEAD = 64
Q_LORA = 384
KV_LORA = 256
ATT_W = MLA_HEADS * V_HEAD
ROPE_THETA = 10000.0
D_INNER = 2048
SSM_HEADDIM = 64
SSM_HEADS = 32
SSM_GROUPS = 4
D_STATE = 128
D_CONV = 5
CONV_CH = D_INNER + 2 * SSM_GROUPS * D_STATE
CHUNK = 128
D_FF = 2816
EPS = 1e-6

LANES = 128
HALO = 8
PAD_ROWS = CHUNK - N_META
HEAD_PAIRS = MLA_HEADS // 2
QK_PAD = LANES
TM = 512
TM_SSM = 256
TQ = 512
TK = 512
FF_CHUNK = 256
NEG = -1e30
VMEM_LIMIT = 56 * 1024 * 1024

F32 = jnp.float32
BF16 = jnp.bfloat16


def _cparams(n_axes):
    return pltpu.CompilerParams(dimension_semantics=("arbitrary",) * n_axes,
                                vmem_limit_bytes=VMEM_LIMIT)


def _resident(shape):
    nd = len(shape)
    return pl.BlockSpec(shape, lambda *_: (0,) * nd, pipeline_mode=pl.Buffered(1))


def _rms(x, g):
    ms = jnp.mean(x * x, axis=-1, keepdims=True)
    return x * lax.rsqrt(ms + EPS) * g


def _sigmoid(x):
    return 1.0 / (1.0 + jnp.exp(-x))


def _dot(a, b):
    return jnp.dot(a, b, preferred_element_type=F32)


def _aligned(x, m):
    return x if isinstance(x, int) else pl.multiple_of(x, m)


def _dot_nt(a, b):
    return lax.dot_general(a, b, (((1,), (1,)), ((), ())), preferred_element_type=F32)


def _ffn_kernel(x_ref, g_ref, win_ref, wout_ref, *rest, final):
    if final:
        fg_ref, o_ref, acc_ref = rest
    else:
        o_ref, acc_ref = rest
    x = x_ref[...]
    h = _rms(x, g_ref[...]).astype(BF16)
    for c in range(D_FF // FF_CHUNK):
        lo = c * FF_CHUNK
        g = _dot(h, win_ref[:, lo:lo + FF_CHUNK])
        u = _dot(h, win_ref[:, D_FF + lo:D_FF + lo + FF_CHUNK])
        a = (g * _sigmoid(g) * u).astype(BF16)
        part = _dot(a, wout_ref[lo:lo + FF_CHUNK, :])
        if c == 0:
            acc_ref[...] = part
        else:
            acc_ref[...] += part
    y = x + 0.5 * acc_ref[...]
    if final:
        y = _rms(y, fg_ref[...])
    o_ref[...] = y


def _ffn(x, gain, w_in, w_out, final_gain=None):
    n = x.shape[0]
    final = final_gain is not None
    row = pl.BlockSpec((TM, D_MODEL), lambda i: (i, 0))
    in_specs = [row, _resident((1, D_MODEL)), _resident((D_MODEL, 2 * D_FF)), _resident((D_FF, D_MODEL))]
    args = [x, gain.reshape(1, D_MODEL), w_in, w_out]
    if final:
        in_specs.append(_resident((1, D_MODEL)))
        args.append(final_gain.reshape(1, D_MODEL))
    return pl.pallas_call(
        functools.partial(_ffn_kernel, final=final),
        grid=(n // TM,),
        in_specs=in_specs,
        out_specs=row,
        out_shape=jax.ShapeDtypeStruct((n, D_MODEL), F32),
        scratch_shapes=[pltpu.VMEM((TM, D_MODEL), F32)],
        compiler_params=_cparams(1),
        name="ffn",
    )(*args)


S_COLS = Q_LORA + KV_LORA + 2 * LANES


def _qkv_kernel(x_ref, g_ref, ws_ref, qn_ref, wqa_ref, wqb_ref, kvn_ref, wk_ref, wvt_ref, cos_ref, sin_ref,
                q_ref, k_ref, vt_ref):
    h = _rms(x_ref[...], g_ref[...]).astype(BF16)
    ps = _dot(h, ws_ref[...])
    cqn = _rms(ps[:, :Q_LORA], qn_ref[...]).astype(BF16)
    ckvn = _rms(ps[:, Q_LORA:Q_LORA + KV_LORA], kvn_ref[...]).astype(BF16)
    kr = ps[:, Q_LORA + KV_LORA:Q_LORA + KV_LORA + LANES]
    krr = ps[:, Q_LORA + KV_LORA + LANES:]
    cos = cos_ref[...]
    sin = sin_ref[...]
    scale = float((QK_NOPE + QK_ROPE) ** -0.5 * np.log2(np.e))
    lane = lax.broadcasted_iota(jnp.int32, (1, LANES), 1)
    cosq = jnp.where(lane < QK_NOPE, scale, cos * scale)
    sinq = sin * scale
    krope = kr * cos + krr * sin
    qa = _dot(cqn, wqa_ref[...])
    qb = _dot(cqn, wqb_ref[...])
    ka = _dot(ckvn, wk_ref[...])
    for hh in range(MLA_HEADS):
        sl = slice(hh * QK_PAD, (hh + 1) * QK_PAD)
        q_ref[hh] = (qa[:, sl] * cosq + qb[:, sl] * sinq).astype(BF16)
        k_ref[hh] = (ka[:, sl] + krope).astype(BF16)
    vt_ref[...] = _dot_nt(wvt_ref[...], ckvn).astype(BF16)


def _qkv(x, gain, ws, qn, wqa, wqb, kvn, wk, wvt, cos_t, sin_t):
    n = x.shape[0]
    hq = MLA_HEADS * QK_PAD
    row = lambda w: pl.BlockSpec((TM, w), lambda i: (i, 0))
    head = pl.BlockSpec((MLA_HEADS, TM, QK_PAD), lambda i: (0, i, 0))
    return pl.pallas_call(
        _qkv_kernel,
        grid=(n // TM,),
        in_specs=[row(D_MODEL), _resident((1, D_MODEL)), _resident((D_MODEL, S_COLS)),
                  _resident((1, Q_LORA)), _resident((Q_LORA, hq)), _resident((Q_LORA, hq)),
                  _resident((1, KV_LORA)), _resident((KV_LORA, hq)), _resident((ATT_W, KV_LORA)),
                  row(LANES), row(LANES)],
        out_specs=[head, head, pl.BlockSpec((ATT_W, TM), lambda i: (0, i))],
        out_shape=[jax.ShapeDtypeStruct((MLA_HEADS, n, QK_PAD), BF16),
                   jax.ShapeDtypeStruct((MLA_HEADS, n, QK_PAD), BF16),
                   jax.ShapeDtypeStruct((ATT_W, n), BF16)],
        compiler_params=_cparams(1),
        name="qkv_proj",
    )(x, gain.reshape(1, D_MODEL), ws, qn.reshape(1, Q_LORA), wqa, wqb, kvn.reshape(1, KV_LORA), wk, wvt,
      cos_t, sin_t)


DT_COLS = 2 * LANES


def _ssm_proj_kernel(x_ref, g_ref, wz_ref, wx_ref, wdt_ref, dtb_ref, wg_ref, z_ref, xbc_ref, dt_ref, gate_ref,
                     *, n_main):
    h = _rms(x_ref[...], g_ref[...]).astype(BF16)
    row = pl.program_id(0) * TM_SSM + lax.broadcasted_iota(jnp.int32, (TM_SSM, 1), 0)
    unused = jnp.logical_and(row >= n_main, ((row - n_main) % CHUNK) < PAD_ROWS)
    z_ref[...] = _dot(h, wz_ref[...])
    xbc_ref[...] = jnp.where(unused, 0.0, _dot(h, wx_ref[...]))
    dtr = _dot(h, wdt_ref[...]) + dtb_ref[...]
    dt = jnp.maximum(dtr, 0.0) + jnp.log(1.0 + jnp.exp(-jnp.abs(dtr)))
    lane = lax.broadcasted_iota(jnp.int32, (1, DT_COLS), 1)
    dead = jnp.logical_or(unused, (lane % LANES) >= SSM_HEADS)
    dt_ref[...] = jnp.where(dead, 0.0, dt)
    gate_ref[...] = _sigmoid(_dot(h, wg_ref[...]))


def _ssm_proj(x, gain, wz, wx, wdt, dtb, wg, n_main):
    n = x.shape[0]
    row = lambda w: pl.BlockSpec((TM_SSM, w), lambda i: (i, 0))
    return pl.pallas_call(
        functools.partial(_ssm_proj_kernel, n_main=n_main),
        grid=(n // TM_SSM,),
        in_specs=[row(D_MODEL), _resident((1, D_MODEL)), _resident((D_MODEL, D_INNER)),
                  _resident((D_MODEL, CONV_CH)), _resident((D_MODEL, DT_COLS)), _resident((1, DT_COLS)),
                  _resident((D_MODEL, 2 * D_MODEL))],
        out_specs=[row(D_INNER), row(CONV_CH), row(DT_COLS), row(2 * D_MODEL)],
        out_shape=[jax.ShapeDtypeStruct((n, D_INNER), F32), jax.ShapeDtypeStruct((n, CONV_CH), F32),
                   jax.ShapeDtypeStruct((n, DT_COLS), F32), jax.ShapeDtypeStruct((n, 2 * D_MODEL), F32)],
        compiler_params=_cparams(1),
        name="ssm_proj",
    )(x, gain.reshape(1, D_MODEL), wz, wx, wdt, dtb, wg)


STRIP = 64


def _scores(hh, q, k, s_ref, base, masked):
    n_keys = k.shape[0]
    s_t = _dot_nt(k, q)
    if masked:
        key_i = lax.broadcasted_iota(jnp.int32, (n_keys, 1), 0)
        s_t = jnp.where(key_i >= PAD_ROWS, s_t, NEG)
    s_ref[hh, pl.ds(base, n_keys), :] = s_t
    return jnp.max(s_t, axis=0, keepdims=True)


def _softmax_pv(hh, vt, m, l, cmax, s_ref, base, p_ref, acc_ref, first):
    n_keys = vt.shape[1]
    tq = s_ref.shape[2]
    m_new = cmax if first else jnp.maximum(m, cmax)
    m_b = jnp.broadcast_to(m_new, (HALO, tq))
    strip = min(STRIP, n_keys)
    part = jnp.zeros((HALO, tq), F32)
    for r in range(n_keys // strip):
        blk = s_ref[hh, pl.ds(base + r * strip, strip), :].reshape(strip // HALO, HALO, tq)
        pb = jnp.exp2(blk - m_b[None])
        part = part + jnp.sum(pb, axis=0)
        p_ref[hh, r * strip:(r + 1) * strip, :] = pb.reshape(strip, tq).astype(BF16)
    psum = jnp.sum(part, axis=0, keepdims=True)
    pv = _dot(vt, p_ref[hh, 0:n_keys, :])
    if first:
        acc_ref[hh] = pv
        return m_new, psum
    alpha = jnp.exp2(m - m_new)
    acc_ref[hh] = alpha * acc_ref[hh] + pv
    return m_new, alpha * l + psum


def _attn_kernel(q_ref, km_ref, kmeta_ref, vtm_ref, vtmeta_ref, prev_ref, o_ref,
                 s0_ref, s1_ref, p0_ref, p1_ref, acc_ref, *, seq):
    del prev_ref
    n_chunks = seq // TK
    qs = [q_ref[0], q_ref[1]]
    s_slots = (s0_ref, s1_ref)
    p_slots = (p0_ref, p1_ref)

    state = []
    for hh in range(2):
        cmax = _scores(hh, qs[hh], kmeta_ref[hh], s1_ref, 0, True)
        m, l = _softmax_pv(hh, vtmeta_ref[...], None, None, cmax, s1_ref, 0, p1_ref, acc_ref, True)
        state.append((m, l))
    state = [state[hh] + (_scores(hh, qs[hh], km_ref[hh, 0:TK, :], s0_ref, 0, False),) for hh in range(2)]

    def step(j, slot, carry, prefetch):
        off = _aligned(j * TK, TK)
        out = []
        for hh in range(2):
            m, l, cmax = carry[hh]
            if prefetch:
                cmax_next = _scores(hh, qs[hh], km_ref[hh, pl.ds(_aligned(off + TK, TK), TK), :],
                                    s_slots[1 - slot], 0, False)
            else:
                cmax_next = cmax
            m, l = _softmax_pv(hh, vtm_ref[:, pl.ds(off, TK)], m, l, cmax, s_slots[slot], 0, p_slots[slot],
                               acc_ref, False)
            out.append((m, l, cmax_next))
        return tuple(out)

    def two_steps(i, carry):
        carry = step(2 * i, 0, carry, True)
        return step(2 * i + 1, 1, carry, True)

    carry = lax.fori_loop(0, n_chunks // 2 - 1, two_steps, tuple(state))
    carry = step(n_chunks - 2, 0, carry, True)
    (m0, l0, _), (m1, l1, _) = step(n_chunks - 1, 1, carry, False)
    dim_i = lax.broadcasted_iota(jnp.int32, (LANES, 1), 0)
    o_t = jnp.where(dim_i < V_HEAD, acc_ref[0] / l0, acc_ref[1] / l1)
    o_ref[...] = o_t.T.astype(BF16)


def _attention(q, k, vt, o_prev, *, n_main, main_off, seq_off, batch, seq, meta_queries):
    n = k.shape[1]
    meta_blk = n_main // CHUNK + seq_off
    kv_blk = main_off // seq
    if meta_queries:
        tq = CHUNK
        grid = (batch, HEAD_PAIRS, 1)
        q_row = lambda s, p, i: meta_blk + s
    else:
        tq = TQ
        grid = (batch, HEAD_PAIRS, seq // TQ)
        q_row = lambda s, p, i: main_off // TQ + s * (seq // TQ) + i
    in_specs = [
        pl.BlockSpec((2, tq, QK_PAD), lambda s, p, i: (p, q_row(s, p, i), 0)),
        pl.BlockSpec((2, seq, QK_PAD), lambda s, p, i: (p, kv_blk + s, 0)),
        pl.BlockSpec((2, CHUNK, QK_PAD), lambda s, p, i: (p, meta_blk + s, 0)),
        pl.BlockSpec((LANES, seq), lambda s, p, i: (p, kv_blk + s)),
        pl.BlockSpec((LANES, CHUNK), lambda s, p, i: (p, meta_blk + s)),
        pl.BlockSpec(memory_space=pl.ANY),
    ]
    return pl.pallas_call(
        functools.partial(_attn_kernel, seq=seq),
        grid=grid,
        in_specs=in_specs,
        out_specs=pl.BlockSpec((tq, LANES), lambda s, p, i: (q_row(s, p, i), p)),
        out_shape=jax.ShapeDtypeStruct((n, ATT_W), BF16),
        scratch_shapes=[pltpu.VMEM((2, TK, tq), F32), pltpu.VMEM((2, TK, tq), F32),
                        pltpu.VMEM((2, TK, tq), BF16), pltpu.VMEM((2, TK, tq), BF16),
                        pltpu.VMEM((2, LANES, tq), F32)],
        input_output_aliases={5: 0},
        compiler_params=_cparams(3),
        name="attn_meta" if meta_queries else "attn_main",
    )(q, k, k, vt, vt, o_prev)


def _ssd_kernel(*refs, n_chunks, backward, finalize):
    if finalize:
        (xbc_ref, hprev_ref, hnext_ref, dt_ref, cw_ref, cb_ref, alog_ref, z_ref, yprev_ref, norm_ref, alias_ref,
         o_ref, state_ref, xext_ref) = refs
    else:
        (xbc_ref, hprev_ref, hnext_ref, dt_ref, cw_ref, cb_ref, alog_ref, dskip_ref, alias_ref,
         o_ref, state_ref, xext_ref) = refs
    del alias_ref
    step = pl.program_id(1)
    c = (n_chunks - 1 - step) if backward else step

    @pl.when(step == 0)
    def _():
        state_ref[...] = jnp.zeros_like(state_ref)

    xext_ref[0:HALO, :] = jnp.where(c == 0, 0.0, hprev_ref[...])
    xext_ref[HALO:HALO + CHUNK, :] = xbc_ref[...]
    xext_ref[HALO + CHUNK:, :] = jnp.where(c == n_chunks - 1, 0.0, hnext_ref[...])
    conv = cb_ref[...]
    for kk in range(D_CONV):
        lo = HALO - D_CONV // 2 + kk
        conv = conv + cw_ref[kk:kk + 1, :] * xext_ref[lo:lo + CHUNK, :]
    xc = conv * _sigmoid(conv)
    xs = xc[:, :D_INNER]

    dt = dt_ref[...]
    a = -jnp.exp(alog_ref[...])
    da = dt * a
    hi = da.astype(BF16)
    r1 = da - hi.astype(F32)
    mid = r1.astype(BF16)
    lo3 = (r1 - mid.astype(F32)).astype(BF16)
    row_i = lax.broadcasted_iota(jnp.int32, (CHUNK, CHUNK), 0)
    col_i = lax.broadcasted_iota(jnp.int32, (CHUNK, CHUNK), 1)
    keep = (col_i >= row_i) if backward else (col_i <= row_i)
    tri = jnp.where(keep, 1.0, 0.0).astype(BF16)
    acs = _dot(tri, hi) + _dot(tri, mid) + _dot(tri, lo3)
    acs_t = acs.T
    dt_t = dt.T
    edge = 0 if backward else CHUNK - 1
    a_end_t = acs_t[:, edge:edge + 1]
    w_t = jnp.exp(a_end_t - acs_t) * dt_t
    lane = lax.broadcasted_iota(jnp.int32, (1, LANES), 1)
    first_half = lane < SSM_HEADDIM

    y_tiles = []
    for g in range(SSM_GROUPS):
        bg = xc[:, D_INNER + g * D_STATE:D_INNER + (g + 1) * D_STATE]
        cg = xc[:, D_INNER + SSM_GROUPS * D_STATE + g * D_STATE:D_INNER + SSM_GROUPS * D_STATE + (g + 1) * D_STATE]
        bg_t = bg.T
        cgb = cg.astype(BF16)
        cb = _dot_nt(cgb, bg.astype(BF16))
        gl = slice(g * 4 * LANES, (g + 1) * 4 * LANES)
        prev = state_ref[:, gl]
        y_off = _dot(cgb, prev.astype(BF16))
        new_tiles, decay_tiles = [], []
        for qq in range(4):
            h1 = g * 8 + qq * 2
            h2 = h1 + 1
            pair = slice(g * 4 * LANES + qq * LANES, g * 4 * LANES + (qq + 1) * LANES)
            x_pair = xs[:, pair]
            x_lo = jnp.where(first_half, x_pair, 0.0).astype(BF16)
            x_hi = jnp.where(first_half, 0.0, x_pair).astype(BF16)
            x_diag = jnp.concatenate([x_lo, x_hi], axis=0)
            ws, bs, cols = [], [], []
            for hh in (h1, h2):
                col = jnp.broadcast_to(acs[:, hh:hh + 1], (CHUNK, CHUNK))
                seg = col - acs_t[hh:hh + 1, :]
                lmat = jnp.exp(jnp.where(keep, seg, NEG))
                ws.append((cb * lmat * dt_t[hh:hh + 1, :]).astype(BF16))
                bs.append((bg_t * w_t[hh:hh + 1, :]).astype(BF16))
                cols.append(col)
            y_diag = _dot(jnp.concatenate(ws, axis=1), x_diag)
            new_tiles.append(_dot(jnp.concatenate(bs, axis=1), x_diag))
            acs_pair = jnp.where(first_half, cols[0], cols[1])
            decay_tiles.append(jnp.exp(acs_pair[edge:edge + 1, :]))
            y_tiles.append(y_diag + y_off[:, qq * LANES:(qq + 1) * LANES] * jnp.exp(acs_pair))
        state_ref[:, gl] = prev * jnp.concatenate(decay_tiles, axis=1) + jnp.concatenate(new_tiles, axis=1)
    y = jnp.concatenate(y_tiles, axis=1)

    if not finalize:
        o_ref[...] = y + xs * dskip_ref[...]
    else:
        z = z_ref[...]
        y = (y + yprev_ref[...]) * (z * _sigmoid(z))
        gw = D_INNER // SSM_GROUPS
        outs = []
        for g in range(SSM_GROUPS):
            yg = y[:, g * gw:(g + 1) * gw]
            outs.append(yg * lax.rsqrt(jnp.mean(yg * yg, axis=-1, keepdims=True) + EPS))
        o_ref[...] = (jnp.concatenate(outs, axis=1) * norm_ref[...]).astype(BF16)


def _ssd_pass(xbc, dt, cw, cb, alog, extra, o_prev, *, n_main, main_off, seq_off, batch, seq, backward, finalize):
    n = xbc.shape[0]
    n_chunks = seq // CHUNK + 1
    halo_per_chunk = CHUNK // HALO
    meta_blk = n_main // CHUNK + seq_off
    main_blk = main_off // CHUNK

    def chunk_of(step):
        return (n_chunks - 1 - step) if backward else step

    def blk(s, step):
        c = chunk_of(step)
        return jnp.where(c == 0, meta_blk + s, main_blk + s * (seq // CHUNK) + c - 1)

    def halo_prev(s, step):
        c = chunk_of(step)
        first_main = (meta_blk + s) * halo_per_chunk + halo_per_chunk - 1
        return jnp.where(c <= 1, first_main, (main_blk + s * (seq // CHUNK) + c - 1) * halo_per_chunk - 1)

    def halo_next(s, step):
        c = chunk_of(step)
        nxt = jnp.minimum(c, n_chunks - 2)
        return (main_blk + s * (seq // CHUNK) + nxt) * halo_per_chunk

    d = 1 if backward else 0
    chunk_rows = lambda w: pl.BlockSpec((CHUNK, w), lambda s, t: (blk(s, t), 0))
    in_specs = [
        chunk_rows(CONV_CH),
        pl.BlockSpec((HALO, CONV_CH), lambda s, t: (halo_prev(s, t), 0)),
        pl.BlockSpec((HALO, CONV_CH), lambda s, t: (halo_next(s, t), 0)),
        pl.BlockSpec((CHUNK, LANES), lambda s, t: (blk(s, t), d)),
        _resident((HALO, CONV_CH)), _resident((1, CONV_CH)), _resident((1, LANES)),
    ]
    args = [xbc, xbc, xbc, dt, cw, cb, alog]
    if finalize:
        z, y_fwd, norm = extra
        in_specs += [chunk_rows(D_INNER), chunk_rows(D_INNER), _resident((1, D_INNER))]
        args += [z, y_fwd, norm]
        out_dtype = BF16
    else:
        (dskip,) = extra
        in_specs += [_resident((1, D_INNER))]
        args += [dskip]
        out_dtype = F32
    in_specs.append(pl.BlockSpec(memory_space=pl.ANY))
    args.append(o_prev)
    return pl.pallas_call(
        functools.partial(_ssd_kernel, n_chunks=n_chunks, backward=backward, finalize=finalize),
        grid=(batch, n_chunks),
        in_specs=in_specs,
        out_specs=chunk_rows(D_INNER),
        out_shape=jax.ShapeDtypeStruct((n, D_INNER), out_dtype),
        scratch_shapes=[pltpu.VMEM((D_STATE, D_INNER), F32), pltpu.VMEM((CHUNK + 2 * HALO, CONV_CH), F32)],
        input_output_aliases={len(args) - 1: 0},
        compiler_params=_cparams(2),
        name="ssd_bwd" if backward else "ssd_fwd",
    )(*args)


def _merge_kernel(oa_ref, om_ref, gate_ref, x_ref, wba_ref, wbm_ref, wo_ref, o_ref):
    y_a = _dot(oa_ref[...], wba_ref[...])
    y_m = _dot(om_ref[...], wbm_ref[...])
    mix = gate_ref[:, :D_MODEL] * y_a + gate_ref[:, D_MODEL:] * y_m
    o_ref[...] = x_ref[...] + _dot(mix.astype(BF16), wo_ref[...])


def _merge(o_a, o_m, gates, x, wba, wbm, wo):
    n = x.shape[0]
    row = lambda w: pl.BlockSpec((TM, w), lambda i: (i, 0))
    return pl.pallas_call(
        _merge_kernel,
        grid=(n // TM,),
        in_specs=[row(ATT_W), row(D_INNER), row(2 * D_MODEL), row(D_MODEL),
                  _resident((ATT_W, D_MODEL)), _resident((D_INNER, D_MODEL)), _resident((D_MODEL, D_MODEL))],
        out_specs=row(D_MODEL),
        out_shape=jax.ShapeDtypeStruct((n, D_MODEL), F32),
        compiler_params=_cparams(1),
        name="merge_out",
    )(o_a, o_m, gates, x, wba, wbm, wo)


def _head_tiles(w, width):
    k = w.shape[0]
    w = w.reshape(k, MLA_HEADS, width)
    return jnp.pad(w, ((0, 0), (0, 0), (0, QK_PAD - width))).reshape(k, MLA_HEADS * QK_PAD)


def _rot_cols(w):
    half = QK_ROPE // 2
    return jnp.concatenate([-w[..., half:], w[..., :half]], axis=-1)


def _layer_weights(l, w_in, w_uq, w_ukv, conv_w, conv_b, a_log, dt_bias, d_skip, w_branch):
    offs = np.cumsum((Q_LORA, KV_LORA, QK_ROPE, D_INNER, CONV_CH, 2 * SSM_HEADS, 2 * D_MODEL))
    wi = w_in[l]
    w_cq, w_ckv, w_kr = wi[:, :offs[0]], wi[:, offs[0]:offs[1]], wi[:, offs[1]:offs[2]]
    w_z, w_xbc = wi[:, offs[2]:offs[3]], wi[:, offs[3]:offs[4]]
    w_dt, w_gate = wi[:, offs[4]:offs[5]], wi[:, offs[5]:offs[6]]
    rope_tile = lambda w: jnp.pad(w, ((0, 0), (QK_NOPE, QK_PAD - QK_NOPE - QK_ROPE)))
    ws = jnp.concatenate([w_cq, w_ckv, rope_tile(w_kr), rope_tile(_rot_cols(w_kr))], axis=1).astype(BF16)
    uq = w_uq[l].reshape(Q_LORA, MLA_HEADS, QK_NOPE + QK_ROPE)
    uq_rot = jnp.concatenate([jnp.zeros_like(uq[..., :QK_NOPE]), _rot_cols(uq[..., QK_NOPE:])], axis=-1)
    wqa = _head_tiles(uq.reshape(Q_LORA, -1), QK_NOPE + QK_ROPE).astype(BF16)
    wqb = _head_tiles(uq_rot.reshape(Q_LORA, -1), QK_NOPE + QK_ROPE).astype(BF16)
    ukv = w_ukv[l].reshape(KV_LORA, MLA_HEADS, QK_NOPE + V_HEAD)
    wk = _head_tiles(ukv[..., :QK_NOPE].reshape(KV_LORA, -1), QK_NOPE).astype(BF16)
    wvt = ukv[..., QK_NOPE:].reshape(KV_LORA, ATT_W).T.astype(BF16)
    dt_tile = lambda w: jnp.pad(w, ((0, 0), (0, LANES - SSM_HEADS)))
    wdt = jnp.concatenate([dt_tile(w_dt[:, :SSM_HEADS]), dt_tile(w_dt[:, SSM_HEADS:])], axis=1).astype(BF16)
    dtb = jnp.concatenate([dt_tile(dt_bias[l][0:1]), dt_tile(dt_bias[l][1:2])], axis=1)
    return dict(
        ws=ws, wqa=wqa, wqb=wqb, wk=wk, wvt=wvt,
        wz=w_z.astype(BF16), wx=w_xbc.astype(BF16), wdt=wdt, dtb=dtb, wg=w_gate.astype(BF16),
        cw=jnp.pad(conv_w[l], ((0, HALO - D_CONV), (0, 0))), cb=conv_b[l].reshape(1, CONV_CH),
        alog=[dt_tile(a_log[l][0:1]), dt_tile(a_log[l][1:2])],
        dskip=jnp.repeat(d_skip[l], SSM_HEADDIM).reshape(1, D_INNER),
        wba=w_branch[l][:ATT_W].astype(BF16), wbm=w_branch[l][ATT_W:].astype(BF16),
    )


def _rope_tables(groups, n_main, n):
    inv = ROPE_THETA ** (-jnp.arange(0, QK_ROPE, 2, dtype=F32) / QK_ROPE)
    pos = [jnp.tile(jnp.arange(N_META, N_META + s, dtype=F32), b) for b, s in groups]
    meta = jnp.arange(CHUNK, dtype=F32) - PAD_ROWS
    pos.append(jnp.tile(meta, (n - n_main) // CHUNK))
    ang = jnp.concatenate(pos)[:, None] * inv[None, :]
    ang = jnp.concatenate([ang, ang], axis=-1)
    place = lambda t: jnp.pad(t, ((0, 0), (QK_NOPE, QK_PAD - QK_NOPE - QK_ROPE)))
    return place(jnp.cos(ang)), place(jnp.sin(ang))


def kernel(x_prompt, x_sample, meta_tokens, ffn1_norm, ffn1_w_in, ffn1_w_out, mix_norm, w_in, q_norm, w_uq, kv_norm, w_ukv, conv_w, conv_b, a_log, dt_bias, d_skip, ssm_norm, w_branch, w_out, ffn2_norm, ffn2_w_in, ffn2_w_out, final_norm):
    xs_in = (x_prompt, x_sample)
    groups = [(x.shape[0], x.shape[1]) for x in xs_in]
    depth = w_in.shape[0]
    n_seq = sum(b for b, _ in groups)
    n_main = sum(b * s for b, s in groups)
    n = n_main + -(-(n_seq * CHUNK) // TM) * TM
    main_offs, seq_offs = [], []
    mo = so = 0
    for b, s in groups:
        assert s % TQ == 0 and s % (2 * TK) == 0 and mo % s == 0, (groups,)
        main_offs.append(mo)
        seq_offs.append(so)
        mo += b * s
        so += b

    meta_block = jnp.concatenate([jnp.zeros((PAD_ROWS, D_MODEL), F32), meta_tokens.astype(F32)], axis=0)
    x = jnp.concatenate([xi.reshape(-1, D_MODEL) for xi in xs_in]
                        + [jnp.tile(meta_block, ((n - n_main) // CHUNK, 1))], axis=0)
    cos_t, sin_t = _rope_tables(groups, n_main, n)

    for l in range(depth):
        lw = _layer_weights(l, w_in, w_uq, w_ukv, conv_w, conv_b, a_log, dt_bias, d_skip, w_branch)
        x = _ffn(x, ffn1_norm[l], ffn1_w_in[l].astype(BF16), ffn1_w_out[l].astype(BF16))

        q, k, vt = _qkv(x, mix_norm[l], lw["ws"], q_norm[l], lw["wqa"], lw["wqb"], kv_norm[l], lw["wk"], lw["wvt"],
                       cos_t, sin_t)
        z, xbc, dt, gates = _ssm_proj(x, mix_norm[l], lw["wz"], lw["wx"], lw["wdt"], lw["dtb"], lw["wg"], n_main)

        o_a = jnp.zeros((n, ATT_W), BF16)
        y_f = jnp.zeros((n, D_INNER), F32)
        o_m = jnp.zeros((n, D_INNER), BF16)
        geo = [dict(n_main=n_main, main_off=main_offs[g], seq_off=seq_offs[g], batch=groups[g][0], seq=groups[g][1])
               for g in range(len(groups))]
        for gg in geo:
            o_a = _attention(q, k, vt, o_a, meta_queries=False, **gg)
            o_a = _attention(q, k, vt, o_a, meta_queries=True, **gg)
            y_f = _ssd_pass(xbc, dt, lw["cw"], lw["cb"], lw["alog"][0], (lw["dskip"],), y_f,
                            backward=False, finalize=False, **gg)
        for gg in geo:
            o_m = _ssd_pass(xbc, dt, lw["cw"], lw["cb"], lw["alog"][1], (z, y_f, ssm_norm[l].reshape(1, D_INNER)),
                            o_m, backward=True, finalize=True, **gg)

        x = _merge(o_a, o_m, gates, x, lw["wba"], lw["wbm"], w_out[l].astype(BF16))
        x = _ffn(x, ffn2_norm[l], ffn2_w_in[l].astype(BF16), ffn2_w_out[l].astype(BF16),
                 final_gain=final_norm if l == depth - 1 else None)

    outs, off = [], 0
    for (b, s), xi in zip(groups, xs_in):
        outs.append(x[off:off + b * s].reshape(b, s, D_MODEL).astype(xi.dtype))
        off += b * s
    return tuple(outs)
```

```python
import functools

import numpy as np
import jax
import jax.numpy as jnp
from jax import lax
from jax.experimental import pallas as pl
from jax.experimental.pallas import tpu as pltpu

D_MODEL = 1024
N_META = 16
MLA_HEADS = 16
QK_NOPE = 64
QK_ROPE = 32
V_HEAD = 64
Q_LORA = 384
KV_LORA = 256
ATT_W = MLA_HEADS * V_HEAD
ROPE_THETA = 10000.0
D_INNER = 2048
SSM_HEADDIM = 64
SSM_HEADS = 32
SSM_GROUPS = 4
D_STATE = 128
D_CONV = 5
CONV_CH = D_INNER + 2 * SSM_GROUPS * D_STATE
CHUNK = 128
D_FF = 2816
EPS = 1e-6

LANES = 128
HALO = 8
PAD_ROWS = CHUNK - N_META
HEAD_PAIRS = MLA_HEADS // 2
QK_PAD = LANES
TM = 512
TM_SSM = 256
TQ = 512
TK = 1024
FF_CHUNK = 256
NEG = -1e30
VMEM_LIMIT = 56 * 1024 * 1024

F32 = jnp.float32
BF16 = jnp.bfloat16


def _cparams(n_axes):
    return pltpu.CompilerParams(dimension_semantics=("arbitrary",) * n_axes,
                                vmem_limit_bytes=VMEM_LIMIT)


def _resident(shape):
    nd = len(shape)
    return pl.BlockSpec(shape, lambda *_: (0,) * nd, pipeline_mode=pl.Buffered(1))


def _rms(x, g):
    ms = jnp.mean(x * x, axis=-1, keepdims=True)
    return x * lax.rsqrt(ms + EPS) * g


def _sigmoid(x):
    return 1.0 / (1.0 + jnp.exp(-x))


def _dot(a, b):
    return jnp.dot(a, b, preferred_element_type=F32)


def _aligned(x, m):
    return x if isinstance(x, int) else pl.multiple_of(x, m)


def _dot_nt(a, b):
    return lax.dot_general(a, b, (((1,), (1,)), ((), ())), preferred_element_type=F32)


def _ffn_kernel(x_ref, g_ref, win_ref, wout_ref, *rest, final):
    if final:
        fg_ref, o_ref, acc_ref = rest
    else:
        o_ref, acc_ref = rest
    x = x_ref[...]
    h = _rms(x, g_ref[...]).astype(BF16)
    for c in range(D_FF // FF_CHUNK):
        lo = c * FF_CHUNK
        g = _dot(h, win_ref[:, lo:lo + FF_CHUNK])
        u = _dot(h, win_ref[:, D_FF + lo:D_FF + lo + FF_CHUNK])
        a = (g * _sigmoid(g) * u).astype(BF16)
        part = _dot(a, wout_ref[lo:lo + FF_CHUNK, :])
        if c == 0:
            acc_ref[...] = part
        else:
            acc_ref[...] += part
    y = x + 0.5 * acc_ref[...]
    if final:
        y = _rms(y, fg_ref[...])
    o_ref[...] = y


def _ffn(x, gain, w_in, w_out, final_gain=None):
    n = x.shape[0]
    final = final_gain is not None
    row = pl.BlockSpec((TM, D_MODEL), lambda i: (i, 0))
    in_specs = [row, _resident((1, D_MODEL)), _resident((D_MODEL, 2 * D_FF)), _resident((D_FF, D_MODEL))]
    args = [x, gain.reshape(1, D_MODEL), w_in, w_out]
    if final:
        in_specs.append(_resident((1, D_MODEL)))
        args.append(final_gain.reshape(1, D_MODEL))
    return pl.pallas_call(
        functools.partial(_ffn_kernel, final=final),
        grid=(n // TM,),
        in_specs=in_specs,
        out_specs=row,
        out_shape=jax.ShapeDtypeStruct((n, D_MODEL), F32),
        scratch_shapes=[pltpu.VMEM((TM, D_MODEL), F32)],
        compiler_params=_cparams(1),
        name="ffn",
    )(*args)


S_COLS = Q_LORA + KV_LORA + 2 * LANES


def _qkv_kernel(x_ref, g_ref, ws_ref, qn_ref, wqa_ref, wqb_ref, kvn_ref, wk_ref, wvt_ref, cos_ref, sin_ref,
                q_ref, k_ref, vt_ref):
    h = _rms(x_ref[...], g_ref[...]).astype(BF16)
    ps = _dot(h, ws_ref[...])
    cqn = _rms(ps[:, :Q_LORA], qn_ref[...]).astype(BF16)
    ckvn = _rms(ps[:, Q_LORA:Q_LORA + KV_LORA], kvn_ref[...]).astype(BF16)
    kr = ps[:, Q_LORA + KV_LORA:Q_LORA + KV_LORA + LANES]
    krr = ps[:, Q_LORA + KV_LORA + LANES:]
    cos = cos_ref[...]
    sin = sin_ref[...]
    scale = float((QK_NOPE + QK_ROPE) ** -0.5 * np.log2(np.e))
    lane = lax.broadcasted_iota(jnp.int32, (1, LANES), 1)
    cosq = jnp.where(lane < QK_NOPE, scale, cos * scale)
    sinq = sin * scale
    krope = kr * cos + krr * sin
    qa = _dot(cqn, wqa_ref[...])
    qb = _dot(cqn, wqb_ref[...])
    ka = _dot(ckvn, wk_ref[...])
    for hh in range(MLA_HEADS):
        sl = slice(hh * QK_PAD, (hh + 1) * QK_PAD)
        q_ref[hh] = (qa[:, sl] * cosq + qb[:, sl] * sinq).astype(BF16)
        k_ref[hh] = (ka[:, sl] + krope).astype(BF16)
    row_i = lax.broadcasted_iota(jnp.int32, (MLA_HEADS * V_ROWS, 1), 0)
    ones_row = jnp.where(row_i % V_ROWS == V_HEAD, 1.0, 0.0)
    vt_ref[...] = (_dot_nt(wvt_ref[...], ckvn) + ones_row).astype(BF16)


def _qkv(x, gain, ws, qn, wqa, wqb, kvn, wk, wvt, cos_t, sin_t):
    n = x.shape[0]
    hq = MLA_HEADS * QK_PAD
    row = lambda w: pl.BlockSpec((TM, w), lambda i: (i, 0))
    head = pl.BlockSpec((MLA_HEADS, TM, QK_PAD), lambda i: (0, i, 0))
    return pl.pallas_call(
        _qkv_kernel,
        grid=(n // TM,),
        in_specs=[row(D_MODEL), _resident((1, D_MODEL)), _resident((D_MODEL, S_COLS)),
                  _resident((1, Q_LORA)), _resident((Q_LORA, hq)), _resident((Q_LORA, hq)),
                  _resident((1, KV_LORA)), _resident((KV_LORA, hq)), _resident((MLA_HEADS * V_ROWS, KV_LORA)),
                  row(LANES), row(LANES)],
        out_specs=[head, head, pl.BlockSpec((MLA_HEADS * V_ROWS, TM), lambda i: (0, i))],
        out_shape=[jax.ShapeDtypeStruct((MLA_HEADS, n, QK_PAD), BF16),
                   jax.ShapeDtypeStruct((MLA_HEADS, n, QK_PAD), BF16),
                   jax.ShapeDtypeStruct((MLA_HEADS * V_ROWS, n), BF16)],
        compiler_params=_cparams(1),
        name="qkv_proj",
    )(x, gain.reshape(1, D_MODEL), ws, qn.reshape(1, Q_LORA), wqa, wqb, kvn.reshape(1, KV_LORA), wk, wvt,
      cos_t, sin_t)


DT_COLS = 2 * LANES


def _ssm_proj_kernel(x_ref, g_ref, wz_ref, wx_ref, wdt_ref, dtb_ref, wg_ref, z_ref, xbc_ref, dt_ref, gate_ref,
                     *, n_main):
    h = _rms(x_ref[...], g_ref[...]).astype(BF16)
    row = pl.program_id(0) * TM_SSM + lax.broadcasted_iota(jnp.int32, (TM_SSM, 1), 0)
    unused = jnp.logical_and(row >= n_main, ((row - n_main) % CHUNK) < PAD_ROWS)
    z_ref[...] = _dot(h, wz_ref[...])
    xbc_ref[...] = jnp.where(unused, 0.0, _dot(h, wx_ref[...]))
    dtr = _dot(h, wdt_ref[...]) + dtb_ref[...]
    dt = jnp.maximum(dtr, 0.0) + jnp.log(1.0 + jnp.exp(-jnp.abs(dtr)))
    lane = lax.broadcasted_iota(jnp.int32, (1, DT_COLS), 1)
    dead = jnp.logical_or(unused, (lane % LANES) >= SSM_HEADS)
    dt_ref[...] = jnp.where(dead, 0.0, dt)
    gate_ref[...] = _sigmoid(_dot(h, wg_ref[...]))


def _ssm_proj(x, gain, wz, wx, wdt, dtb, wg, n_main):
    n = x.shape[0]
    row = lambda w: pl.BlockSpec((TM_SSM, w), lambda i: (i, 0))
    return pl.pallas_call(
        functools.partial(_ssm_proj_kernel, n_main=n_main),
        grid=(n // TM_SSM,),
        in_specs=[row(D_MODEL), _resident((1, D_MODEL)), _resident((D_MODEL, D_INNER)),
                  _resident((D_MODEL, CONV_CH)), _resident((D_MODEL, DT_COLS)), _resident((1, DT_COLS)),
                  _resident((D_MODEL, 2 * D_MODEL))],
        out_specs=[row(D_INNER), row(CONV_CH), row(DT_COLS), row(2 * D_MODEL)],
        out_shape=[jax.ShapeDtypeStruct((n, D_INNER), F32), jax.ShapeDtypeStruct((n, CONV_CH), F32),
                   jax.ShapeDtypeStruct((n, DT_COLS), F32), jax.ShapeDtypeStruct((n, 2 * D_MODEL), F32)],
        compiler_params=_cparams(1),
        name="ssm_proj",
    )(x, gain.reshape(1, D_MODEL), wz, wx, wdt, dtb, wg)


STRIP = 32
V_ROWS = 80


def _scores(hh, q_ref, k, s_ref, base, masked):
    n_keys = k.shape[0]
    s_t = _dot_nt(k, q_ref[hh])
    if masked:
        key_i = lax.broadcasted_iota(jnp.int32, (n_keys, 1), 0)
        s_t = jnp.where(key_i >= PAD_ROWS, s_t, NEG)
    s_ref[hh, pl.ds(base, n_keys), :] = s_t
    return jnp.max(s_t, axis=0, keepdims=True)


def _softmax_pv(hh, vts, m, cmax, s_ref, p_ref, acc_ref):
    tq = s_ref.shape[2]
    n_keys = sum(vt.shape[1] for vt in vts)
    m_new = cmax if m is None else jnp.maximum(m, cmax)
    m_b = jnp.broadcast_to(m_new, (STRIP, tq))
    for r in range(n_keys // STRIP):
        rows = slice(r * STRIP, (r + 1) * STRIP)
        p_ref[hh, rows, :] = jnp.exp2(s_ref[hh, rows, :] - m_b).astype(BF16)
    pv, lo = None, 0
    for vt in vts:
        part = _dot(vt, p_ref[hh, lo:lo + vt.shape[1], :])
        pv = part if pv is None else pv + part
        lo += vt.shape[1]
    if m is None:
        acc_ref[hh] = pv
    else:
        acc_ref[hh] = jnp.exp2(m - m_new) * acc_ref[hh] + pv
    return m_new


def _attn_kernel(q_ref, km_ref, kmeta_ref, vtm_ref, vtmeta_ref, *rest, seq):
    o_ref, s0_ref, s1_ref, p0_ref, p1_ref, acc_ref = rest[-6:]
    n_chunks = seq // TK
    s_slots = (s0_ref, s1_ref)
    p_slots = (p0_ref, p1_ref)
    head_rows = lambda hh: slice(hh * V_ROWS, (hh + 1) * V_ROWS)

    carry = []
    for hh in range(2):
        cmax = jnp.maximum(_scores(hh, q_ref, kmeta_ref[hh], s0_ref, 0, True),
                           _scores(hh, q_ref, km_ref[hh, 0:TK, :], s0_ref, CHUNK, False))
        cmax_next = _scores(hh, q_ref, km_ref[hh, TK:2 * TK, :], s1_ref, 0, False)
        m = _softmax_pv(hh, [vtmeta_ref[head_rows(hh), :], vtm_ref[head_rows(hh), 0:TK]], None, cmax,
                        s0_ref, p0_ref, acc_ref)
        carry.append((m, cmax_next))

    def step(j, slot, carry, prefetch):
        off = _aligned(j * TK, TK)
        out = []
        for hh in range(2):
            m, cmax = carry[hh]
            if prefetch:
                cmax_next = _scores(hh, q_ref, km_ref[hh, pl.ds(_aligned(off + TK, TK), TK), :],
                                    s_slots[1 - slot], 0, False)
            else:
                cmax_next = cmax
            m = _softmax_pv(hh, [vtm_ref[head_rows(hh), pl.ds(off, TK)]], m, cmax, s_slots[slot], p_slots[slot],
                            acc_ref)
            out.append((m, cmax_next))
        return tuple(out)

    def two_steps(i, carry):
        carry = step(2 * i + 1, 1, carry, True)
        return step(2 * i + 2, 0, carry, True)

    carry = lax.fori_loop(0, (n_chunks - 2) // 2, two_steps, tuple(carry))
    step(n_chunks - 1, 1, carry, False)
    o_t = jnp.concatenate([acc_ref[hh, 0:V_HEAD, :] / acc_ref[hh, V_HEAD:V_HEAD + 1, :] for hh in range(2)], axis=0)
    o_ref[...] = o_t.T.astype(BF16)


def _attention(q, k, vt, o_prev, *, n_main, main_off, seq_off, batch, seq, meta_queries):
    n = k.shape[1]
    meta_blk = n_main // CHUNK + seq_off
    kv_blk = main_off // seq
    if meta_queries:
        tq = CHUNK
        grid = (batch, HEAD_PAIRS, 1)
        q_row = lambda s, p, i: meta_blk + s
    else:
        tq = TQ
        grid = (batch, HEAD_PAIRS, seq // TQ)
        q_row = lambda s, p, i: main_off // TQ + s * (seq // TQ) + i
    in_specs = [
        pl.BlockSpec((2, tq, QK_PAD), lambda s, p, i: (p, q_row(s, p, i), 0)),
        pl.BlockSpec((2, seq, QK_PAD), lambda s, p, i: (p, kv_blk + s, 0)),
        pl.BlockSpec((2, CHUNK, QK_PAD), lambda s, p, i: (p, meta_blk + s, 0)),
        pl.BlockSpec((2 * V_ROWS, seq), lambda s, p, i: (p, kv_blk + s)),
        pl.BlockSpec((2 * V_ROWS, CHUNK), lambda s, p, i: (p, meta_blk + s)),
    ]
    args = [q, k, k, vt, vt]
    aliases = {}
    if o_prev is not None:
        in_specs.append(pl.BlockSpec(memory_space=pl.ANY))
        args.append(o_prev)
        aliases = {len(args) - 1: 0}
    return pl.pallas_call(
        functools.partial(_attn_kernel, seq=seq),
        grid=grid,
        in_specs=in_specs,
        out_specs=pl.BlockSpec((tq, LANES), lambda s, p, i: (q_row(s, p, i), p)),
        out_shape=jax.ShapeDtypeStruct((n, ATT_W), BF16),
        scratch_shapes=[pltpu.VMEM((2, TK + CHUNK, tq), F32), pltpu.VMEM((2, TK, tq), F32),
                        pltpu.VMEM((2, TK + CHUNK, tq), BF16), pltpu.VMEM((2, TK, tq), BF16),
                        pltpu.VMEM((2, V_ROWS, tq), F32)],
        input_output_aliases=aliases,
        compiler_params=_cparams(3),
        name="attn_meta" if meta_queries else "attn_main",
    )(*args)


def _ssd_kernel(*refs, n_chunks, n_alias, backward):
    if backward:
        xc_ref, dt_ref, alog_ref, z_ref, yprev_ref, norm_ref = refs[:6]
        o_ref, state_ref = refs[6 + n_alias:]
    else:
        xbc_ref, hprev_ref, hnext_ref, dt_ref, cw_ref, cb_ref, alog_ref, dskip_ref = refs[:8]
        o_ref, xc_ref, state_ref, xext_ref = refs[8 + n_alias:]
    step = pl.program_id(1)
    c = (n_chunks - 1 - step) if backward else step

    @pl.when(step == 0)
    def _():
        state_ref[...] = jnp.zeros_like(state_ref)

    if not backward:
        xext_ref[0:HALO, :] = jnp.where(c == 0, 0.0, hprev_ref[...])
        xext_ref[HALO:HALO + CHUNK, :] = xbc_ref[...]
        xext_ref[HALO + CHUNK:, :] = jnp.where(c == n_chunks - 1, 0.0, hnext_ref[...])
        for t in range(CONV_CH // LANES):
            cols = slice(t * LANES, (t + 1) * LANES)
            conv = cb_ref[:, cols]
            for kk in range(D_CONV):
                lo = HALO - D_CONV // 2 + kk
                conv = conv + cw_ref[kk:kk + 1, cols] * xext_ref[lo:lo + CHUNK, cols]
            xc = conv * _sigmoid(conv)
            xc_ref[:, cols] = xc.astype(BF16)
            if t < D_INNER // LANES:
                o_ref[:, cols] = xc * dskip_ref[:, cols]

    dt = dt_ref[...]
    a = -jnp.exp(alog_ref[...]) * float(np.log2(np.e))
    da = dt * a
    hi = da.astype(BF16)
    r1 = da - hi.astype(F32)
    mid = r1.astype(BF16)
    lo3 = (r1 - mid.astype(F32)).astype(BF16)
    row_i = lax.broadcasted_iota(jnp.int32, (CHUNK, CHUNK), 0)
    col_i = lax.broadcasted_iota(jnp.int32, (CHUNK, CHUNK), 1)
    keep = (col_i >= row_i) if backward else (col_i <= row_i)
    tri = jnp.where(keep, 1.0, 0.0).astype(BF16)
    acs = _dot(tri, hi) + _dot(tri, mid) + _dot(tri, lo3)
    acs_t = acs.T
    dt_t = dt.T
    edge = 0 if backward else CHUNK - 1
    a_end_t = acs_t[:, edge:edge + 1]
    w_t = jnp.exp2(a_end_t - acs_t) * dt_t
    lane = lax.broadcasted_iota(jnp.int32, (1, LANES), 1)
    first_half = lane < SSM_HEADDIM
    zero = jnp.zeros((), BF16)

    c_off = D_INNER + SSM_GROUPS * D_STATE
    y_all = []
    for g in range(SSM_GROUPS):
        bgb = xc_ref[:, D_INNER + g * D_STATE:D_INNER + (g + 1) * D_STATE]
        cgb = xc_ref[:, c_off + g * D_STATE:c_off + (g + 1) * D_STATE]
        bg_t = bgb.astype(F32).T
        cb = _dot_nt(cgb, bgb)
        gl = slice(g * 4 * LANES, (g + 1) * 4 * LANES)
        prev = state_ref[:, gl]
        y_off = _dot(cgb, prev.astype(BF16))
        new_tiles, decay_tiles, y_tiles = [], [], []
        for qq in range(4):
            h1 = g * 8 + qq * 2
            h2 = h1 + 1
            pair = slice(g * 4 * LANES + qq * LANES, g * 4 * LANES + (qq + 1) * LANES)
            x_pair = xc_ref[:, pair]
            x_lo = jnp.where(first_half, x_pair, zero)
            x_hi = jnp.where(first_half, zero, x_pair)
            x_diag = jnp.concatenate([x_lo, x_hi], axis=0)
            ws, bs, cols = [], [], []
            for hh in (h1, h2):
                col = jnp.broadcast_to(acs[:, hh:hh + 1], (CHUNK, CHUNK))
                seg = col - acs_t[hh:hh + 1, :]
                lmat = jnp.exp2(jnp.where(keep, seg, NEG))
                ws.append((cb * lmat * dt_t[hh:hh + 1, :]).astype(BF16))
                bs.append((bg_t * w_t[hh:hh + 1, :]).astype(BF16))
                cols.append(col)
            y_diag = _dot(jnp.concatenate(ws, axis=1), x_diag)
            new_tiles.append(_dot(jnp.concatenate(bs, axis=1), x_diag))
            acs_pair = jnp.where(first_half, cols[0], cols[1])
            decay_tiles.append(jnp.exp2(acs_pair[edge:edge + 1, :]))
            y_pair = y_diag + y_off[:, qq * LANES:(qq + 1) * LANES] * jnp.exp2(acs_pair)
            if backward:
                y_tiles.append(y_pair)
            else:
                o_ref[:, pair] += y_pair
        state_ref[:, gl] = prev * jnp.concatenate(decay_tiles, axis=1) + jnp.concatenate(new_tiles, axis=1)
        if backward:
            y_all.append(jnp.concatenate(y_tiles, axis=1))
    if backward:
        z = z_ref[...]
        y = (jnp.concatenate(y_all, axis=1) + yprev_ref[...]) * (z * _sigmoid(z))
        gw = D_INNER // SSM_GROUPS
        outs = []
        for g in range(SSM_GROUPS):
            yg = y[:, g * gw:(g + 1) * gw]
            outs.append(yg * lax.rsqrt(jnp.mean(yg * yg, axis=-1, keepdims=True) + EPS))
        o_ref[...] = (jnp.concatenate(outs, axis=1) * norm_ref[...]).astype(BF16)


def _ssd_pass(src, dt, alog, extra, prev_outs, *, n_main, main_off, seq_off, batch, seq, backward):
    n = src.shape[0]
    n_chunks = seq // CHUNK + 1
    halo_per_chunk = CHUNK // HALO
    meta_blk = n_main // CHUNK + seq_off
    main_blk = main_off // CHUNK

    def chunk_of(step):
        return (n_chunks - 1 - step) if backward else step

    def blk(s, step):
        c = chunk_of(step)
        return jnp.where(c == 0, meta_blk + s, main_blk + s * (seq // CHUNK) + c - 1)

    def halo_prev(s, step):
        c = chunk_of(step)
        first_main = (meta_blk + s) * halo_per_chunk + halo_per_chunk - 1
        return jnp.where(c <= 1, first_main, (main_blk + s * (seq // CHUNK) + c - 1) * halo_per_chunk - 1)

    def halo_next(s, step):
        c = chunk_of(step)
        nxt = jnp.minimum(c, n_chunks - 2)
        return (main_blk + s * (seq // CHUNK) + nxt) * halo_per_chunk

    d = 1 if backward else 0
    chunk_rows = lambda w: pl.BlockSpec((CHUNK, w), lambda s, t: (blk(s, t), 0))
    dt_spec = pl.BlockSpec((CHUNK, LANES), lambda s, t: (blk(s, t), d))
    scratch = [pltpu.VMEM((D_STATE, D_INNER), F32)]
    if backward:
        z, y_fwd, norm = extra
        in_specs = [chunk_rows(CONV_CH), dt_spec, _resident((1, LANES)),
                    chunk_rows(D_INNER), chunk_rows(D_INNER), _resident((1, D_INNER))]
        args = [src, dt, alog, z, y_fwd, norm]
        out_specs = [chunk_rows(D_INNER)]
        out_shape = [jax.ShapeDtypeStruct((n, D_INNER), BF16)]
    else:
        cw, cb, dskip = extra
        in_specs = [chunk_rows(CONV_CH),
                    pl.BlockSpec((HALO, CONV_CH), lambda s, t: (halo_prev(s, t), 0)),
                    pl.BlockSpec((HALO, CONV_CH), lambda s, t: (halo_next(s, t), 0)),
                    dt_spec, _resident((HALO, CONV_CH)), _resident((1, CONV_CH)), _resident((1, LANES)),
                    _resident((1, D_INNER))]
        args = [src, src, src, dt, cw, cb, alog, dskip]
        out_specs = [chunk_rows(D_INNER), chunk_rows(CONV_CH)]
        out_shape = [jax.ShapeDtypeStruct((n, D_INNER), F32), jax.ShapeDtypeStruct((n, CONV_CH), BF16)]
        scratch.append(pltpu.VMEM((CHUNK + 2 * HALO, CONV_CH), F32))
    aliases = {}
    if prev_outs is not None:
        for i, prev in enumerate(prev_outs):
            in_specs.append(pl.BlockSpec(memory_space=pl.ANY))
            args.append(prev)
            aliases[len(args) - 1] = i
    return pl.pallas_call(
        functools.partial(_ssd_kernel, n_chunks=n_chunks, n_alias=len(aliases), backward=backward),
        grid=(batch, n_chunks),
        in_specs=in_specs,
        out_specs=out_specs,
        out_shape=out_shape,
        scratch_shapes=scratch,
        input_output_aliases=aliases,
        compiler_params=_cparams(2),
        name="ssd_bwd" if backward else "ssd_fwd",
    )(*args)


def _merge_kernel(oa_ref, om_ref, gate_ref, x_ref, wba_ref, wbm_ref, wo_ref, o_ref):
    y_a = _dot(oa_ref[...], wba_ref[...])
    y_m = _dot(om_ref[...], wbm_ref[...])
    mix = gate_ref[:, :D_MODEL] * y_a + gate_ref[:, D_MODEL:] * y_m
    o_ref[...] = x_ref[...] + _dot(mix.astype(BF16), wo_ref[...])


def _merge(o_a, o_m, gates, x, wba, wbm, wo):
    n = x.shape[0]
    row = lambda w: pl.BlockSpec((TM, w), lambda i: (i, 0))
    return pl.pallas_call(
        _merge_kernel,
        grid=(n // TM,),
        in_specs=[row(ATT_W), row(D_INNER), row(2 * D_MODEL), row(D_MODEL),
                  _resident((ATT_W, D_MODEL)), _resident((D_INNER, D_MODEL)), _resident((D_MODEL, D_MODEL))],
        out_specs=row(D_MODEL),
        out_shape=jax.ShapeDtypeStruct((n, D_MODEL), F32),
        compiler_params=_cparams(1),
        name="merge_out",
    )(o_a, o_m, gates, x, wba, wbm, wo)


def _head_tiles(w, width):
    k = w.shape[0]
    w = w.reshape(k, MLA_HEADS, width)
    return jnp.pad(w, ((0, 0), (0, 0), (0, QK_PAD - width))).reshape(k, MLA_HEADS * QK_PAD)


def _rot_cols(w):
    half = QK_ROPE // 2
    return jnp.concatenate([-w[..., half:], w[..., :half]], axis=-1)


def _layer_weights(l, w_in, w_uq, w_ukv, conv_w, conv_b, a_log, dt_bias, d_skip, w_branch):
    offs = np.cumsum((Q_LORA, KV_LORA, QK_ROPE, D_INNER, CONV_CH, 2 * SSM_HEADS, 2 * D_MODEL))
    wi = w_in[l]
    w_cq, w_ckv, w_kr = wi[:, :offs[0]], wi[:, offs[0]:offs[1]], wi[:, offs[1]:offs[2]]
    w_z, w_xbc = wi[:, offs[2]:offs[3]], wi[:, offs[3]:offs[4]]
    w_dt, w_gate = wi[:, offs[4]:offs[5]], wi[:, offs[5]:offs[6]]
    rope_tile = lambda w: jnp.pad(w, ((0, 0), (QK_NOPE, QK_PAD - QK_NOPE - QK_ROPE)))
    ws = jnp.concatenate([w_cq, w_ckv, rope_tile(w_kr), rope_tile(_rot_cols(w_kr))], axis=1).astype(BF16)
    uq = w_uq[l].reshape(Q_LORA, MLA_HEADS, QK_NOPE + QK_ROPE)
    uq_rot = jnp.concatenate([jnp.zeros_like(uq[..., :QK_NOPE]), _rot_cols(uq[..., QK_NOPE:])], axis=-1)
    wqa = _head_tiles(uq.reshape(Q_LORA, -1), QK_NOPE + QK_ROPE).astype(BF16)
    wqb = _head_tiles(uq_rot.reshape(Q_LORA, -1), QK_NOPE + QK_ROPE).astype(BF16)
    ukv = w_ukv[l].reshape(KV_LORA, MLA_HEADS, QK_NOPE + V_HEAD)
    wk = _head_tiles(ukv[..., :QK_NOPE].reshape(KV_LORA, -1), QK_NOPE).astype(BF16)
    wvt = jnp.pad(jnp.transpose(ukv[..., QK_NOPE:], (1, 2, 0)), ((0, 0), (0, V_ROWS - V_HEAD), (0, 0)))
    wvt = wvt.reshape(MLA_HEADS * V_ROWS, KV_LORA).astype(BF16)
    dt_tile = lambda w: jnp.pad(w, ((0, 0), (0, LANES - SSM_HEADS)))
    wdt = jnp.concatenate([dt_tile(w_dt[:, :SSM_HEADS]), dt_tile(w_dt[:, SSM_HEADS:])], axis=1).astype(BF16)
    dtb = jnp.concatenate([dt_tile(dt_bias[l][0:1]), dt_tile(dt_bias[l][1:2])], axis=1)
    return dict(
        ws=ws, wqa=wqa, wqb=wqb, wk=wk, wvt=wvt,
        wz=w_z.astype(BF16), wx=w_xbc.astype(BF16), wdt=wdt, dtb=dtb, wg=w_gate.astype(BF16),
        cw=jnp.pad(conv_w[l], ((0, HALO - D_CONV), (0, 0))), cb=conv_b[l].reshape(1, CONV_CH),
        alog=[dt_tile(a_log[l][0:1]), dt_tile(a_log[l][1:2])],
        dskip=jnp.repeat(d_skip[l], SSM_HEADDIM).reshape(1, D_INNER),
        wba=w_branch[l][:ATT_W].astype(BF16), wbm=w_branch[l][ATT_W:].astype(BF16),
    )


def _rope_tables(groups, n_main, n):
    inv = ROPE_THETA ** (-jnp.arange(0, QK_ROPE, 2, dtype=F32) / QK_ROPE)
    pos = [jnp.tile(jnp.arange(N_META, N_META + s, dtype=F32), b) for b, s in groups]
    meta = jnp.arange(CHUNK, dtype=F32) - PAD_ROWS
    pos.append(jnp.tile(meta, (n - n_main) // CHUNK))
    ang = jnp.concatenate(pos)[:, None] * inv[None, :]
    ang = jnp.concatenate([ang, ang], axis=-1)
    place = lambda t: jnp.pad(t, ((0, 0), (QK_NOPE, QK_PAD - QK_NOPE - QK_ROPE)))
    return place(jnp.cos(ang)), place(jnp.sin(ang))


def kernel(x_prompt, x_sample, meta_tokens, ffn1_norm, ffn1_w_in, ffn1_w_out, mix_norm, w_in, q_norm, w_uq, kv_norm, w_ukv, conv_w, conv_b, a_log, dt_bias, d_skip, ssm_norm, w_branch, w_out, ffn2_norm, ffn2_w_in, ffn2_w_out, final_norm):
    xs_in = (x_prompt, x_sample)
    groups = [(x.shape[0], x.shape[1]) for x in xs_in]
    depth = w_in.shape[0]
    n_seq = sum(b for b, _ in groups)
    n_main = sum(b * s for b, s in groups)
    n = n_main + -(-(n_seq * CHUNK) // TM) * TM
    main_offs, seq_offs = [], []
    mo = so = 0
    for b, s in groups:
        assert s % TQ == 0 and s % (2 * TK) == 0 and mo % s == 0, (groups,)
        main_offs.append(mo)
        seq_offs.append(so)
        mo += b * s
        so += b

    meta_block = jnp.concatenate([jnp.zeros((PAD_ROWS, D_MODEL), F32), meta_tokens.astype(F32)], axis=0)
    x = jnp.concatenate([xi.reshape(-1, D_MODEL) for xi in xs_in]
                        + [jnp.tile(meta_block, ((n - n_main) // CHUNK, 1))], axis=0)
    cos_t, sin_t = _rope_tables(groups, n_main, n)

    for l in range(depth):
        lw = _layer_weights(l, w_in, w_uq, w_ukv, conv_w, conv_b, a_log, dt_bias, d_skip, w_branch)
        x = _ffn(x, ffn1_norm[l], ffn1_w_in[l].astype(BF16), ffn1_w_out[l].astype(BF16))

        q, k, vt = _qkv(x, mix_norm[l], lw["ws"], q_norm[l], lw["wqa"], lw["wqb"], kv_norm[l], lw["wk"], lw["wvt"],
                       cos_t, sin_t)
        z, xbc, dt, gates = _ssm_proj(x, mix_norm[l], lw["wz"], lw["wx"], lw["wdt"], lw["dtb"], lw["wg"], n_main)

        o_a = fwd = o_m = None
        geo = [dict(n_main=n_main, main_off=main_offs[g], seq_off=seq_offs[g], batch=groups[g][0], seq=groups[g][1])
               for g in range(len(groups))]
        for gg in geo:
            o_a = _attention(q, k, vt, o_a, meta_queries=False, **gg)
            o_a = _attention(q, k, vt, o_a, meta_queries=True, **gg)
            fwd = _ssd_pass(xbc, dt, lw["alog"][0], (lw["cw"], lw["cb"], lw["dskip"]), fwd, backward=False, **gg)
        y_f, xc = fwd
        for gg in geo:
            o_m = _ssd_pass(xc, dt, lw["alog"][1], (z, y_f, ssm_norm[l].reshape(1, D_INNER)), o_m,
                            backward=True, **gg)
        o_m = o_m[0]

        x = _merge(o_a, o_m, gates, x, lw["wba"], lw["wbm"], w_out[l].astype(BF16))
        x = _ffn(x, ffn2_norm[l], ffn2_w_in[l].astype(BF16), ffn2_w_out[l].astype(BF16),
                 final_gain=final_norm if l == depth - 1 else None)

    outs, off = [], 0
    for (b, s), xi in zip(groups, xs_in):
        outs.append(x[off:off + b * s].reshape(b, s, D_MODEL).astype(xi.dtype))
        off += b * s
    return tuple(outs)
```

```python
import functools

import numpy as np
import jax
import jax.numpy as jnp
from jax import lax
from jax.experimental import pallas as pl
from jax.experimental.pallas import tpu as pltpu

D_MODEL = 1024
N_META = 16
MLA_HEADS = 16
QK_NOPE = 64
QK_ROPE = 32
V_HEAD = 64
Q_LORA = 384
KV_LORA = 256
ATT_W = MLA_HEADS * V_HEAD
ROPE_THETA = 10000.0
D_INNER = 2048
SSM_HEADDIM = 64
SSM_HEADS = 32
SSM_GROUPS = 4
D_STATE = 128
D_CONV = 5
CONV_CH = D_INNER + 2 * SSM_GROUPS * D_STATE
CHUNK = 128
D_FF = 2816
EPS = 1e-6

LANES = 128
HALO = 8
PAD_ROWS = CHUNK - N_META
HEAD_PAIRS = MLA_HEADS // 2
QK_PAD = LANES
TM = 512
TM_SSM = 256
TQ = 512
TK = 1024
FF_CHUNK = 256
NEG = -1e30
VMEM_LIMIT = 56 * 1024 * 1024

F32 = jnp.float32
BF16 = jnp.bfloat16


def _cparams(n_axes):
    return pltpu.CompilerParams(dimension_semantics=("arbitrary",) * n_axes,
                                vmem_limit_bytes=VMEM_LIMIT)


def _resident(shape):
    nd = len(shape)
    return pl.BlockSpec(shape, lambda *_: (0,) * nd, pipeline_mode=pl.Buffered(1))


def _rms(x, g):
    ms = jnp.mean(x * x, axis=-1, keepdims=True)
    return x * lax.rsqrt(ms + EPS) * g


def _sigmoid(x):
    return 1.0 / (1.0 + jnp.exp(-x))


def _dot(a, b):
    return jnp.dot(a, b, preferred_element_type=F32)


def _aligned(x, m):
    return x if isinstance(x, int) else pl.multiple_of(x, m)


def _dot_nt(a, b):
    return lax.dot_general(a, b, (((1,), (1,)), ((), ())), preferred_element_type=F32)


def _ffn_kernel(*refs, tiles, entry, final):
    n_in = len(tiles) + 1 if entry else 1
    x_refs = refs[:n_in]
    g_ref, win_ref, wout_ref = refs[n_in:n_in + 3]
    rest = refs[n_in + 3:]
    if final:
        fg_ref = rest[0]
        o_refs = rest[1:-1]
    else:
        (o_ref,) = rest[:-1]
    acc_ref = rest[-1]
    i = pl.program_id(0)
    x = x_refs[-1][...]
    if entry:
        for (lo, hi), r in zip(reversed(tiles), reversed(x_refs[:-1])):
            x = jnp.where(i < hi, r[...], x)
    h = _rms(x, g_ref[...]).astype(BF16)
    for c in range(D_FF // FF_CHUNK):
        lo = c * FF_CHUNK
        g = _dot(h, win_ref[:, lo:lo + FF_CHUNK])
        u = _dot(h, win_ref[:, D_FF + lo:D_FF + lo + FF_CHUNK])
        a = (g * _sigmoid(g) * u).astype(BF16)
        part = _dot(a, wout_ref[lo:lo + FF_CHUNK, :])
        if c == 0:
            acc_ref[...] = part
        else:
            acc_ref[...] += part
    y = x + 0.5 * acc_ref[...]
    if not final:
        o_ref[...] = y
    else:
        y = _rms(y, fg_ref[...])
        for (lo, hi), r in zip(tiles, o_refs):
            @pl.when(jnp.logical_and(i >= lo, i < hi))
            def _(r=r):
                r[...] = y


def _ffn(xs, n, tiles, gain, w_in, w_out, *, entry=False, final_gain=None):
    final = final_gain is not None
    row = pl.BlockSpec((TM, D_MODEL), lambda i: (i, 0))
    group_row = lambda lo, hi: pl.BlockSpec((TM, D_MODEL), lambda i: (jnp.clip(i - lo, 0, hi - lo - 1), 0))
    if entry:
        in_specs = [group_row(lo, hi) for lo, hi in tiles] + [_resident((TM, D_MODEL))]
    else:
        in_specs = [row]
    in_specs += [_resident((1, D_MODEL)), _resident((D_MODEL, 2 * D_FF)), _resident((D_FF, D_MODEL))]
    args = list(xs) + [gain.reshape(1, D_MODEL), w_in, w_out]
    if final:
        in_specs.append(_resident((1, D_MODEL)))
        args.append(final_gain.reshape(1, D_MODEL))
        out_specs = [group_row(lo, hi) for lo, hi in tiles]
        out_shape = [jax.ShapeDtypeStruct(((hi - lo) * TM, D_MODEL), F32) for lo, hi in tiles]
    else:
        out_specs = row
        out_shape = jax.ShapeDtypeStruct((n, D_MODEL), F32)
    return pl.pallas_call(
        functools.partial(_ffn_kernel, tiles=tiles, entry=entry, final=final),
        grid=(n // TM,),
        in_specs=in_specs,
        out_specs=out_specs,
        out_shape=out_shape,
        scratch_shapes=[pltpu.VMEM((TM, D_MODEL), F32)],
        compiler_params=_cparams(1),
        name="ffn",
    )(*args)


S_COLS = Q_LORA + KV_LORA + 2 * LANES


def _qkv_kernel(x_ref, g_ref, ws_ref, qn_ref, wqa_ref, wqb_ref, kvn_ref, wk_ref, wvt_ref, cos_ref, sin_ref,
                q_ref, k_ref, vt_ref):
    h = _rms(x_ref[...], g_ref[...]).astype(BF16)
    ps = _dot(h, ws_ref[...])
    cqn = _rms(ps[:, :Q_LORA], qn_ref[...]).astype(BF16)
    ckvn = _rms(ps[:, Q_LORA:Q_LORA + KV_LORA], kvn_ref[...]).astype(BF16)
    kr = ps[:, Q_LORA + KV_LORA:Q_LORA + KV_LORA + LANES]
    krr = ps[:, Q_LORA + KV_LORA + LANES:]
    cos = cos_ref[...]
    sin = sin_ref[...]
    scale = float((QK_NOPE + QK_ROPE) ** -0.5 * np.log2(np.e))
    lane = lax.broadcasted_iota(jnp.int32, (1, LANES), 1)
    cosq = jnp.where(lane < QK_NOPE, scale, cos * scale)
    sinq = sin * scale
    krope = kr * cos + krr * sin
    qa = _dot(cqn, wqa_ref[...])
    qb = _dot(cqn, wqb_ref[...])
    ka = _dot(ckvn, wk_ref[...])
    for hh in range(MLA_HEADS):
        sl = slice(hh * QK_PAD, (hh + 1) * QK_PAD)
        q_ref[hh] = (qa[:, sl] * cosq + qb[:, sl] * sinq).astype(BF16)
        k_ref[hh] = (ka[:, sl] + krope).astype(BF16)
    row_i = lax.broadcasted_iota(jnp.int32, (MLA_HEADS * V_ROWS, 1), 0)
    ones_row = jnp.where(row_i % V_ROWS == V_HEAD, 1.0, 0.0)
    vt_ref[...] = (_dot_nt(wvt_ref[...], ckvn) + ones_row).astype(BF16)


def _qkv(x, gain, ws, qn, wqa, wqb, kvn, wk, wvt, cos_t, sin_t):
    n = x.shape[0]
    hq = MLA_HEADS * QK_PAD
    row = lambda w: pl.BlockSpec((TM, w), lambda i: (i, 0))
    head = pl.BlockSpec((MLA_HEADS, TM, QK_PAD), lambda i: (0, i, 0))
    return pl.pallas_call(
        _qkv_kernel,
        grid=(n // TM,),
        in_specs=[row(D_MODEL), _resident((1, D_MODEL)), _resident((D_MODEL, S_COLS)),
                  _resident((1, Q_LORA)), _resident((Q_LORA, hq)), _resident((Q_LORA, hq)),
                  _resident((1, KV_LORA)), _resident((KV_LORA, hq)), _resident((MLA_HEADS * V_ROWS, KV_LORA)),
                  row(LANES), row(LANES)],
        out_specs=[head, head, pl.BlockSpec((MLA_HEADS * V_ROWS, TM), lambda i: (0, i))],
        out_shape=[jax.ShapeDtypeStruct((MLA_HEADS, n, QK_PAD), BF16),
                   jax.ShapeDtypeStruct((MLA_HEADS, n, QK_PAD), BF16),
                   jax.ShapeDtypeStruct((MLA_HEADS * V_ROWS, n), BF16)],
        compiler_params=_cparams(1),
        name="qkv_proj",
    )(x, gain.reshape(1, D_MODEL), ws, qn.reshape(1, Q_LORA), wqa, wqb, kvn.reshape(1, KV_LORA), wk, wvt,
      cos_t, sin_t)


DT_COLS = 2 * LANES


def _ssm_proj_kernel(tbl_ref, x_ref, xprev_ref, xnext_ref, g_ref, wz_ref, wx_ref, wdt_ref, dtb_ref, wg_ref,
                     cw_ref, cb_ref, dskip_ref, *rest, tm, meta):
    z_ref, xc_ref, skip_ref, dt_ref, gate_ref, ext_ref = rest[-6:]
    i = pl.program_id(0)
    hx = _rms(jnp.concatenate([x_ref[...], xprev_ref[...], xnext_ref[...]], axis=0), g_ref[...]).astype(BF16)
    h = hx[0:tm]
    if meta:
        row = lax.broadcasted_iota(jnp.int32, (tm, 1), 0)
        unused = row < PAD_ROWS
    xbc = _dot(hx, wx_ref[...])
    body = jnp.where(unused, 0.0, xbc[0:tm]) if meta else xbc[0:tm]
    ext_ref[0:HALO, :] = jnp.where(tbl_ref[2, i] != 0, xbc[tm:tm + HALO], 0.0)
    ext_ref[HALO:HALO + tm, :] = body
    ext_ref[HALO + tm:, :] = jnp.where(tbl_ref[3, i] != 0, xbc[tm + HALO:], 0.0)
    for t in range(CONV_CH // LANES):
        cols = slice(t * LANES, (t + 1) * LANES)
        conv = cb_ref[:, cols]
        for kk in range(D_CONV):
            lo = HALO - D_CONV // 2 + kk
            conv = conv + cw_ref[kk:kk + 1, cols] * ext_ref[lo:lo + tm, cols]
        xc = conv * _sigmoid(conv)
        xc_ref[:, cols] = xc.astype(BF16)
        if t < D_INNER // LANES:
            skip_ref[:, cols] = xc * dskip_ref[:, cols]
    z_ref[...] = _dot(h, wz_ref[...])
    gate_ref[...] = _sigmoid(_dot(h, wg_ref[...]))
    dtr = _dot(h, wdt_ref[...]) + dtb_ref[...]
    dt = jnp.maximum(dtr, 0.0) + jnp.log(1.0 + jnp.exp(-jnp.abs(dtr)))
    lane = lax.broadcasted_iota(jnp.int32, (1, DT_COLS), 1)
    dead = (lane % LANES) >= SSM_HEADS
    if meta:
        dead = jnp.logical_or(unused, dead)
    dt_ref[...] = jnp.where(dead, 0.0, dt)


def _ssm_proj(x, gain, weights, tbl, prev_outs, *, tm, first_blk, meta):
    n = x.shape[0]
    n_tiles = tbl.shape[1]
    row = lambda w: pl.BlockSpec((tm, w), lambda i, tbl: (first_blk + i, 0))
    widths = (D_INNER, CONV_CH, D_INNER, DT_COLS, 2 * D_MODEL)
    dtypes = (F32, BF16, F32, F32, F32)
    const = lambda shape: pl.BlockSpec(shape, lambda i, tbl: (0,) * len(shape), pipeline_mode=pl.Buffered(1))
    in_specs = [row(D_MODEL),
                pl.BlockSpec((HALO, D_MODEL), lambda i, tbl: (tbl[0, i], 0)),
                pl.BlockSpec((HALO, D_MODEL), lambda i, tbl: (tbl[1, i], 0)),
                const((1, D_MODEL)), const((D_MODEL, D_INNER)), const((D_MODEL, CONV_CH)),
                const((D_MODEL, DT_COLS)), const((1, DT_COLS)), const((D_MODEL, 2 * D_MODEL)),
                const((HALO, CONV_CH)), const((1, CONV_CH)), const((1, D_INNER))]
    args = [x, x, x, gain.reshape(1, D_MODEL)] + list(weights)
    aliases = {}
    if prev_outs is not None:
        for k, prev in enumerate(prev_outs):
            in_specs.append(pl.BlockSpec(memory_space=pl.ANY))
            args.append(prev)
            aliases[len(args)] = k
    return pl.pallas_call(
        functools.partial(_ssm_proj_kernel, tm=tm, meta=meta),
        grid_spec=pltpu.PrefetchScalarGridSpec(
            num_scalar_prefetch=1, grid=(n_tiles,), in_specs=in_specs,
            out_specs=[row(w) for w in widths],
            scratch_shapes=[pltpu.VMEM((tm + 2 * HALO, CONV_CH), F32)]),
        out_shape=[jax.ShapeDtypeStruct((n, w), d) for w, d in zip(widths, dtypes)],
        input_output_aliases=aliases,
        compiler_params=_cparams(1),
        name="ssm_proj_meta" if meta else "ssm_proj",
    )(tbl, *args)


def _halo_tables(groups, main_offs, seq_offs, n_main, n):
    main, meta = [], []
    for (b, s), mo, so in zip(groups, main_offs, seq_offs):
        for q in range(b):
            for t in range(s // TM_SSM):
                r0 = mo + q * s + t * TM_SSM
                prev = (n_main + (so + q + 1) * CHUNK) // HALO - 1 if t == 0 else r0 // HALO - 1
                last = t == s // TM_SSM - 1
                main.append((prev, r0 // HALO if last else (r0 + TM_SSM) // HALO, 1, 0 if last else 1))
            meta.append((0, (mo + q * s) // HALO, 0, 1))
    meta += [(0, 0, 0, 0)] * ((n - n_main) // CHUNK - len(meta))
    return (jnp.asarray(np.array(main, np.int32).T), jnp.asarray(np.array(meta, np.int32).T))


STRIP = 32
V_ROWS = 80


def _scores(hh, q_ref, k, s_ref, base, masked):
    n_keys = k.shape[0]
    s_t = _dot_nt(k, q_ref[hh])
    if masked:
        key_i = lax.broadcasted_iota(jnp.int32, (n_keys, 1), 0)
        s_t = jnp.where(key_i >= PAD_ROWS, s_t, NEG)
    s_ref[hh, pl.ds(base, n_keys), :] = s_t
    return jnp.max(s_t, axis=0, keepdims=True)


def _probs(hh, n_keys, m, cmax, s_ref, p_ref):
    tq = s_ref.shape[2]
    m_new = cmax if m is None else jnp.maximum(m, cmax)
    m_b = jnp.broadcast_to(m_new, (STRIP, tq))
    for r in range(n_keys // STRIP):
        rows = slice(r * STRIP, (r + 1) * STRIP)
        p_ref[hh, rows, :] = jnp.exp2(s_ref[hh, rows, :] - m_b).astype(BF16)
    return m_new, (None if m is None else jnp.exp2(m - m_new))


def _accumulate(hh, vts, alpha, p_ref, acc_ref):
    pv, lo = None, 0
    for vt in vts:
        part = _dot(vt, p_ref[hh, lo:lo + vt.shape[1], :])
        pv = part if pv is None else pv + part
        lo += vt.shape[1]
    acc_ref[hh] = pv if alpha is None else alpha * acc_ref[hh] + pv


def _attn_kernel(q_ref, km_ref, kmeta_ref, vtm_ref, vtmeta_ref, *rest, seq):
    o_ref, s0_ref, s1_ref, p0_ref, p1_ref, acc_ref = rest[-6:]
    tk = s1_ref.shape[1]
    n_chunks = seq // tk
    s_slots = (s0_ref, s1_ref)
    p_slots = (p0_ref, p1_ref)
    head_rows = lambda hh: slice(hh * V_ROWS, (hh + 1) * V_ROWS)

    carry = []
    for hh in range(2):
        cmax = jnp.maximum(_scores(hh, q_ref, kmeta_ref[hh], s0_ref, 0, True),
                           _scores(hh, q_ref, km_ref[hh, 0:tk, :], s0_ref, CHUNK, False))
        cmax_next = _scores(hh, q_ref, km_ref[hh, tk:2 * tk, :], s1_ref, 0, False)
        m, _ = _probs(hh, tk + CHUNK, None, cmax, s0_ref, p0_ref)
        _accumulate(hh, [vtmeta_ref[head_rows(hh), :], vtm_ref[head_rows(hh), 0:tk]], None, p0_ref, acc_ref)
        carry.append((m, cmax_next))

    def step(j, slot, carry, prefetch):
        off = _aligned(j * tk, tk)
        out = []
        for hh in range(2):
            m, cmax = carry[hh]
            if prefetch:
                cmax_next = _scores(hh, q_ref, km_ref[hh, pl.ds(_aligned(off + tk, tk), tk), :],
                                    s_slots[1 - slot], 0, False)
            else:
                cmax_next = cmax
            m, alpha = _probs(hh, tk, m, cmax, s_slots[slot], p_slots[slot])
            _accumulate(hh, [vtm_ref[head_rows(hh), pl.ds(off, tk)]], alpha, p_slots[slot], acc_ref)
            out.append((m, cmax_next))
        return tuple(out)

    def two_steps(i, carry):
        carry = step(2 * i + 1, 1, carry, True)
        return step(2 * i + 2, 0, carry, True)

    carry = lax.fori_loop(0, (n_chunks - 2) // 2, two_steps, tuple(carry))
    step(n_chunks - 1, 1, carry, False)
    o_t = jnp.concatenate([acc_ref[hh, 0:V_HEAD, :] / acc_ref[hh, V_HEAD:V_HEAD + 1, :] for hh in range(2)], axis=0)
    o_ref[...] = o_t.T.astype(BF16)


def _attention(q, k, vt, o_prev, *, n_main, main_off, seq_off, batch, seq, meta_queries):
    n = k.shape[1]
    meta_blk = n_main // CHUNK + seq_off
    kv_blk = main_off // seq
    if meta_queries:
        tq, tk = CHUNK, seq // 2
        grid = (batch, HEAD_PAIRS, 1)
        q_row = lambda s, p, i: meta_blk + s
    else:
        tq, tk = TQ, TK
        grid = (batch, HEAD_PAIRS, seq // TQ)
        q_row = lambda s, p, i: main_off // TQ + s * (seq // TQ) + i
    in_specs = [
        pl.BlockSpec((2, tq, QK_PAD), lambda s, p, i: (p, q_row(s, p, i), 0)),
        pl.BlockSpec((2, seq, QK_PAD), lambda s, p, i: (p, kv_blk + s, 0)),
        pl.BlockSpec((2, CHUNK, QK_PAD), lambda s, p, i: (p, meta_blk + s, 0)),
        pl.BlockSpec((2 * V_ROWS, seq), lambda s, p, i: (p, kv_blk + s)),
        pl.BlockSpec((2 * V_ROWS, CHUNK), lambda s, p, i: (p, meta_blk + s)),
    ]
    args = [q, k, k, vt, vt]
    aliases = {}
    if o_prev is not None:
        in_specs.append(pl.BlockSpec(memory_space=pl.ANY))
        args.append(o_prev)
        aliases = {len(args) - 1: 0}
    return pl.pallas_call(
        functools.partial(_attn_kernel, seq=seq),
        grid=grid,
        in_specs=in_specs,
        out_specs=pl.BlockSpec((tq, LANES), lambda s, p, i: (q_row(s, p, i), p)),
        out_shape=jax.ShapeDtypeStruct((n, ATT_W), BF16),
        scratch_shapes=[pltpu.VMEM((2, tk + CHUNK, tq), F32), pltpu.VMEM((2, tk, tq), F32),
                        pltpu.VMEM((2, tk + CHUNK, tq), BF16), pltpu.VMEM((2, tk, tq), BF16),
                        pltpu.VMEM((2, V_ROWS, tq), F32)],
        input_output_aliases=aliases,
        compiler_params=_cparams(3),
        name="attn_meta" if meta_queries else "attn_main",
    )(*args)


def _ssd_kernel(*refs, n_alias, backward):
    if backward:
        xc_ref, dt_ref, alog_ref, z_ref, yprev_ref, norm_ref = refs[:6]
        o_ref, state_ref = refs[6 + n_alias:]
    else:
        xc_ref, dt_ref, alog_ref, skip_ref = refs[:4]
        o_ref, state_ref = refs[4 + n_alias:]

    @pl.when(pl.program_id(1) == 0)
    def _():
        state_ref[...] = jnp.zeros_like(state_ref)

    dt = dt_ref[...]
    a = -jnp.exp(alog_ref[...]) * float(np.log2(np.e))
    da = dt * a
    hi = da.astype(BF16)
    r1 = da - hi.astype(F32)
    mid = r1.astype(BF16)
    lo3 = (r1 - mid.astype(F32)).astype(BF16)
    row_i = lax.broadcasted_iota(jnp.int32, (CHUNK, CHUNK), 0)
    col_i = lax.broadcasted_iota(jnp.int32, (CHUNK, CHUNK), 1)
    keep = (col_i >= row_i) if backward else (col_i <= row_i)
    tri = jnp.where(keep, 1.0, 0.0).astype(BF16)
    acs = _dot(tri, hi) + _dot(tri, mid) + _dot(tri, lo3)
    acs_t = acs.T
    dt_t = dt.T
    edge = 0 if backward else CHUNK - 1
    a_end_t = acs_t[:, edge:edge + 1]
    w_t = jnp.exp2(a_end_t - acs_t) * dt_t
    lane = lax.broadcasted_iota(jnp.int32, (1, LANES), 1)
    first_half = lane < SSM_HEADDIM
    zero = jnp.zeros((), BF16)

    c_off = D_INNER + SSM_GROUPS * D_STATE
    y_all = []
    for g in range(SSM_GROUPS):
        bgb = xc_ref[:, D_INNER + g * D_STATE:D_INNER + (g + 1) * D_STATE]
        cgb = xc_ref[:, c_off + g * D_STATE:c_off + (g + 1) * D_STATE]
        bg_t = bgb.astype(F32).T
        cb = _dot_nt(cgb, bgb)
        gl = slice(g * 4 * LANES, (g + 1) * 4 * LANES)
        prev = state_ref[:, gl]
        y_off = _dot(cgb, prev.astype(BF16))
        new_tiles, decay_tiles, y_tiles = [], [], []
        for qq in range(4):
            h1 = g * 8 + qq * 2
            h2 = h1 + 1
            pair = slice(g * 4 * LANES + qq * LANES, g * 4 * LANES + (qq + 1) * LANES)
            x_pair = xc_ref[:, pair]
            x_lo = jnp.where(first_half, x_pair, zero)
            x_hi = jnp.where(first_half, zero, x_pair)
            x_diag = jnp.concatenate([x_lo, x_hi], axis=0)
            ws, bs, cols = [], [], []
            for hh in (h1, h2):
                col = jnp.broadcast_to(acs[:, hh:hh + 1], (CHUNK, CHUNK))
                seg = col - acs_t[hh:hh + 1, :]
                lmat = jnp.exp2(jnp.where(keep, seg, NEG))
                ws.append((cb * lmat * dt_t[hh:hh + 1, :]).astype(BF16))
                bs.append((bg_t * w_t[hh:hh + 1, :]).astype(BF16))
                cols.append(col)
            y_diag = _dot(jnp.concatenate(ws, axis=1), x_diag)
            new_tiles.append(_dot(jnp.concatenate(bs, axis=1), x_diag))
            acs_pair = jnp.where(first_half, cols[0], cols[1])
            decay_tiles.append(jnp.exp2(acs_pair[edge:edge + 1, :]))
            y_tiles.append(y_diag + y_off[:, qq * LANES:(qq + 1) * LANES] * jnp.exp2(acs_pair))
        state_ref[:, gl] = prev * jnp.concatenate(decay_tiles, axis=1) + jnp.concatenate(new_tiles, axis=1)
        y_all.append(jnp.concatenate(y_tiles, axis=1))
    if not backward:
        o_ref[...] = jnp.concatenate(y_all, axis=1) + skip_ref[...]
    else:
        z = z_ref[...]
        y = (jnp.concatenate(y_all, axis=1) + yprev_ref[...]) * (z * _sigmoid(z))
        gw = D_INNER // SSM_GROUPS
        outs = []
        for g in range(SSM_GROUPS):
            yg = y[:, g * gw:(g + 1) * gw]
            outs.append(yg * lax.rsqrt(jnp.mean(yg * yg, axis=-1, keepdims=True) + EPS))
        o_ref[...] = (jnp.concatenate(outs, axis=1) * norm_ref[...]).astype(BF16)


def _ssd_pass(xc, dt, alog, extra, o_prev, *, n_main, main_off, seq_off, batch, seq, backward):
    n = xc.shape[0]
    n_chunks = seq // CHUNK + 1
    meta_blk = n_main // CHUNK + seq_off
    main_blk = main_off // CHUNK

    def blk(s, step):
        c = (n_chunks - 1 - step) if backward else step
        return jnp.where(c == 0, meta_blk + s, main_blk + s * (seq // CHUNK) + c - 1)

    d = 1 if backward else 0
    chunk_rows = lambda w: pl.BlockSpec((CHUNK, w), lambda s, t: (blk(s, t), 0))
    dt_spec = pl.BlockSpec((CHUNK, LANES), lambda s, t: (blk(s, t), d))
    if backward:
        z, y_fwd, norm = extra
        in_specs = [chunk_rows(CONV_CH), dt_spec, _resident((1, LANES)),
                    chunk_rows(D_INNER), chunk_rows(D_INNER), _resident((1, D_INNER))]
        args = [xc, dt, alog, z, y_fwd, norm]
    else:
        (skip,) = extra
        in_specs = [chunk_rows(CONV_CH), dt_spec, _resident((1, LANES)), chunk_rows(D_INNER)]
        args = [xc, dt, alog, skip]
    aliases = {}
    if o_prev is not None:
        in_specs.append(pl.BlockSpec(memory_space=pl.ANY))
        args.append(o_prev)
        aliases = {len(args) - 1: 0}
    return pl.pallas_call(
        functools.partial(_ssd_kernel, n_alias=len(aliases), backward=backward),
        grid=(batch, n_chunks),
        in_specs=in_specs,
        out_specs=chunk_rows(D_INNER),
        out_shape=jax.ShapeDtypeStruct((n, D_INNER), BF16 if backward else F32),
        scratch_shapes=[pltpu.VMEM((D_STATE, D_INNER), F32)],
        input_output_aliases=aliases,
        compiler_params=_cparams(2),
        name="ssd_bwd" if backward else "ssd_fwd",
    )(*args)


def _merge_kernel(oa_ref, om_ref, gate_ref, x_ref, wba_ref, wbm_ref, wo_ref, o_ref):
    y_a = _dot(oa_ref[...], wba_ref[...])
    y_m = _dot(om_ref[...], wbm_ref[...])
    mix = gate_ref[:, :D_MODEL] * y_a + gate_ref[:, D_MODEL:] * y_m
    o_ref[...] = x_ref[...] + _dot(mix.astype(BF16), wo_ref[...])


def _merge(o_a, o_m, gates, x, wba, wbm, wo):
    n = x.shape[0]
    row = lambda w: pl.BlockSpec((TM, w), lambda i: (i, 0))
    return pl.pallas_call(
        _merge_kernel,
        grid=(n // TM,),
        in_specs=[row(ATT_W), row(D_INNER), row(2 * D_MODEL), row(D_MODEL),
                  _resident((ATT_W, D_MODEL)), _resident((D_INNER, D_MODEL)), _resident((D_MODEL, D_MODEL))],
        out_specs=row(D_MODEL),
        out_shape=jax.ShapeDtypeStruct((n, D_MODEL), F32),
        compiler_params=_cparams(1),
        name="merge_out",
    )(o_a, o_m, gates, x, wba, wbm, wo)


def _head_tiles(w, width):
    k = w.shape[0]
    w = w.reshape(k, MLA_HEADS, width)
    return jnp.pad(w, ((0, 0), (0, 0), (0, QK_PAD - width))).reshape(k, MLA_HEADS * QK_PAD)


def _rot_cols(w):
    half = QK_ROPE // 2
    return jnp.concatenate([-w[..., half:], w[..., :half]], axis=-1)


def _layer_weights(l, w_in, w_uq, w_ukv, conv_w, conv_b, a_log, dt_bias, d_skip, w_branch):
    offs = np.cumsum((Q_LORA, KV_LORA, QK_ROPE, D_INNER, CONV_CH, 2 * SSM_HEADS, 2 * D_MODEL))
    wi = w_in[l]
    w_cq, w_ckv, w_kr = wi[:, :offs[0]], wi[:, offs[0]:offs[1]], wi[:, offs[1]:offs[2]]
    w_z, w_xbc = wi[:, offs[2]:offs[3]], wi[:, offs[3]:offs[4]]
    w_dt, w_gate = wi[:, offs[4]:offs[5]], wi[:, offs[5]:offs[6]]
    rope_tile = lambda w: jnp.pad(w, ((0, 0), (QK_NOPE, QK_PAD - QK_NOPE - QK_ROPE)))
    ws = jnp.concatenate([w_cq, w_ckv, rope_tile(w_kr), rope_tile(_rot_cols(w_kr))], axis=1).astype(BF16)
    uq = w_uq[l].reshape(Q_LORA, MLA_HEADS, QK_NOPE + QK_ROPE)
    uq_rot = jnp.concatenate([jnp.zeros_like(uq[..., :QK_NOPE]), _rot_cols(uq[..., QK_NOPE:])], axis=-1)
    wqa = _head_tiles(uq.reshape(Q_LORA, -1), QK_NOPE + QK_ROPE).astype(BF16)
    wqb = _head_tiles(uq_rot.reshape(Q_LORA, -1), QK_NOPE + QK_ROPE).astype(BF16)
    ukv = w_ukv[l].reshape(KV_LORA, MLA_HEADS, QK_NOPE + V_HEAD)
    wk = _head_tiles(ukv[..., :QK_NOPE].reshape(KV_LORA, -1), QK_NOPE).astype(BF16)
    wvt = jnp.pad(jnp.transpose(ukv[..., QK_NOPE:], (1, 2, 0)), ((0, 0), (0, V_ROWS - V_HEAD), (0, 0)))
    wvt = wvt.reshape(MLA_HEADS * V_ROWS, KV_LORA).astype(BF16)
    dt_tile = lambda w: jnp.pad(w, ((0, 0), (0, LANES - SSM_HEADS)))
    wdt = jnp.concatenate([dt_tile(w_dt[:, :SSM_HEADS]), dt_tile(w_dt[:, SSM_HEADS:])], axis=1).astype(BF16)
    dtb = jnp.concatenate([dt_tile(dt_bias[l][0:1]), dt_tile(dt_bias[l][1:2])], axis=1)
    return dict(
        ws=ws, wqa=wqa, wqb=wqb, wk=wk, wvt=wvt,
        wz=w_z.astype(BF16), wx=w_xbc.astype(BF16), wdt=wdt, dtb=dtb, wg=w_gate.astype(BF16),
        cw=jnp.pad(conv_w[l], ((0, HALO - D_CONV), (0, 0))), cb=conv_b[l].reshape(1, CONV_CH),
        alog=[dt_tile(a_log[l][0:1]), dt_tile(a_log[l][1:2])],
        dskip=jnp.repeat(d_skip[l], SSM_HEADDIM).reshape(1, D_INNER),
        wba=w_branch[l][:ATT_W].astype(BF16), wbm=w_branch[l][ATT_W:].astype(BF16),
    )


def _rope_tables(groups, n_main, n):
    inv = ROPE_THETA ** (-jnp.arange(0, QK_ROPE, 2, dtype=F32) / QK_ROPE)
    pos = [jnp.tile(jnp.arange(N_META, N_META + s, dtype=F32), b) for b, s in groups]
    meta = jnp.arange(CHUNK, dtype=F32) - PAD_ROWS
    pos.append(jnp.tile(meta, (n - n_main) // CHUNK))
    ang = jnp.concatenate(pos)[:, None] * inv[None, :]
    ang = jnp.concatenate([ang, ang], axis=-1)
    place = lambda t: jnp.pad(t, ((0, 0), (QK_NOPE, QK_PAD - QK_NOPE - QK_ROPE)))
    return place(jnp.cos(ang)), place(jnp.sin(ang))


def kernel(x_prompt, x_sample, meta_tokens, ffn1_norm, ffn1_w_in, ffn1_w_out, mix_norm, w_in, q_norm, w_uq, kv_norm, w_ukv, conv_w, conv_b, a_log, dt_bias, d_skip, ssm_norm, w_branch, w_out, ffn2_norm, ffn2_w_in, ffn2_w_out, final_norm):
    xs_in = (x_prompt, x_sample)
    groups = [(x.shape[0], x.shape[1]) for x in xs_in]
    depth = w_in.shape[0]
    n_seq = sum(b for b, _ in groups)
    n_main = sum(b * s for b, s in groups)
    n = n_main + -(-(n_seq * CHUNK) // TM) * TM
    main_offs, seq_offs = [], []
    mo = so = 0
    for b, s in groups:
        assert s % TQ == 0 and s % (2 * TK) == 0 and mo % s == 0, (groups,)
        main_offs.append(mo)
        seq_offs.append(so)
        mo += b * s
        so += b

    meta_block = jnp.concatenate([jnp.zeros((PAD_ROWS, D_MODEL), F32), meta_tokens.astype(F32)], axis=0)
    meta_tile = jnp.tile(meta_block, (TM // CHUNK, 1))
    tiles = [(mo // TM, (mo + b * s) // TM) for (b, s), mo in zip(groups, main_offs)]
    x = [xi.reshape(-1, D_MODEL).astype(F32) for xi in xs_in] + [meta_tile]
    cos_t, sin_t = _rope_tables(groups, n_main, n)
    tbl_main, tbl_meta = _halo_tables(groups, main_offs, seq_offs, n_main, n)

    for l in range(depth):
        lw = _layer_weights(l, w_in, w_uq, w_ukv, conv_w, conv_b, a_log, dt_bias, d_skip, w_branch)
        x = _ffn(x if l == 0 else [x], n, tiles, ffn1_norm[l], ffn1_w_in[l].astype(BF16),
                 ffn1_w_out[l].astype(BF16), entry=l == 0)

        q, k, vt = _qkv(x, mix_norm[l], lw["ws"], q_norm[l], lw["wqa"], lw["wqb"], kv_norm[l], lw["wk"], lw["wvt"],
                       cos_t, sin_t)
        ssm_w = (lw["wz"], lw["wx"], lw["wdt"], lw["dtb"], lw["wg"], lw["cw"], lw["cb"], lw["dskip"])
        proj = _ssm_proj(x, mix_norm[l], ssm_w, tbl_main, None, tm=TM_SSM, first_blk=0, meta=False)
        z, xc, skip, dt, gates = _ssm_proj(x, mix_norm[l], ssm_w, tbl_meta, proj, tm=CHUNK,
                                           first_blk=n_main // CHUNK, meta=True)

        o_a = y_f = o_m = None
        geo = [dict(n_main=n_main, main_off=main_offs[g], seq_off=seq_offs[g], batch=groups[g][0], seq=groups[g][1])
               for g in range(len(groups))]
        for gg in geo:
            o_a = _attention(q, k, vt, o_a, meta_queries=False, **gg)
            o_a = _attention(q, k, vt, o_a, meta_queries=True, **gg)
            y_f = _ssd_pass(xc, dt, lw["alog"][0], (skip,), y_f, backward=False, **gg)
        for gg in geo:
            o_m = _ssd_pass(xc, dt, lw["alog"][1], (z, y_f, ssm_norm[l].reshape(1, D_INNER)), o_m,
                            backward=True, **gg)

        x = _merge(o_a, o_m, gates, x, lw["wba"], lw["wbm"], w_out[l].astype(BF16))
        x = _ffn([x], n, tiles, ffn2_norm[l], ffn2_w_in[l].astype(BF16), ffn2_w_out[l].astype(BF16),
                 final_gain=final_norm if l == depth - 1 else None)

    return tuple(y.reshape(b, s, D_MODEL).astype(xi.dtype) for y, (b, s), xi in zip(x, groups, xs_in))
```

```python
import functools

import numpy as np
import jax
import jax.numpy as jnp
from jax import lax
from jax.experimental import pallas as pl
from jax.experimental.pallas import tpu as pltpu

D_MODEL = 1024
N_META = 16
MLA_HEADS = 16
QK_NOPE = 64
QK_ROPE = 32
V_HEAD = 64
Q_LORA = 384
KV_LORA = 256
ATT_W = MLA_HEADS * V_HEAD
ROPE_THETA = 10000.0
D_INNER = 2048
SSM_HEADDIM = 64
SSM_HEADS = 32
SSM_GROUPS = 4
D_STATE = 128
D_CONV = 5
CONV_CH = D_INNER + 2 * SSM_GROUPS * D_STATE
CHUNK = 128
D_FF = 2816
EPS = 1e-6

LANES = 128
HALO = 8
PAD_ROWS = CHUNK - N_META
HEAD_PAIRS = MLA_HEADS // 2
QK_PAD = LANES
TM = 512
TM_SSM = 256
TQ = 512
TK = 1024
FF_CHUNK = 256
NEG = -1e30
VMEM_LIMIT = 56 * 1024 * 1024

F32 = jnp.float32
BF16 = jnp.bfloat16


def _cparams(n_axes):
    return pltpu.CompilerParams(dimension_semantics=("arbitrary",) * n_axes,
                                vmem_limit_bytes=VMEM_LIMIT)


def _resident(shape):
    nd = len(shape)
    return pl.BlockSpec(shape, lambda *_: (0,) * nd, pipeline_mode=pl.Buffered(1))


def _rms(x, g):
    ms = jnp.mean(x * x, axis=-1, keepdims=True)
    return x * lax.rsqrt(ms + EPS) * g


def _sigmoid(x):
    return 1.0 / (1.0 + jnp.exp(-x))


def _dot(a, b):
    return jnp.dot(a, b, preferred_element_type=F32)


def _aligned(x, m):
    return x if isinstance(x, int) else pl.multiple_of(x, m)


def _dot_nt(a, b):
    return lax.dot_general(a, b, (((1,), (1,)), ((), ())), preferred_element_type=F32)


def _ffn_kernel(*refs, tiles, entry, final):
    n_in = len(tiles) + 1 if entry else 1
    x_refs = refs[:n_in]
    g_ref, win_ref, wout_ref = refs[n_in:n_in + 3]
    rest = refs[n_in + 3:]
    if final:
        fg_ref = rest[0]
        o_refs = rest[1:-1]
    else:
        (o_ref,) = rest[:-1]
    acc_ref = rest[-1]
    i = pl.program_id(0)
    x = x_refs[-1][...]
    if entry:
        for (lo, hi), r in zip(reversed(tiles), reversed(x_refs[:-1])):
            x = jnp.where(i < hi, r[...], x)
    h = _rms(x, g_ref[...]).astype(BF16)
    for c in range(D_FF // FF_CHUNK):
        lo = c * FF_CHUNK
        g = _dot(h, win_ref[:, lo:lo + FF_CHUNK])
        u = _dot(h, win_ref[:, D_FF + lo:D_FF + lo + FF_CHUNK])
        a = (g * _sigmoid(g) * u).astype(BF16)
        part = _dot(a, wout_ref[lo:lo + FF_CHUNK, :])
        if c == 0:
            acc_ref[...] = part
        else:
            acc_ref[...] += part
    y = x + 0.5 * acc_ref[...]
    if not final:
        o_ref[...] = y
    else:
        y = _rms(y, fg_ref[...])
        for (lo, hi), r in zip(tiles, o_refs):
            @pl.when(jnp.logical_and(i >= lo, i < hi))
            def _(r=r):
                r[...] = y


def _ffn(xs, n, tiles, gain, w_in, w_out, *, entry=False, final_gain=None):
    final = final_gain is not None
    row = pl.BlockSpec((TM, D_MODEL), lambda i: (i, 0))
    group_row = lambda lo, hi: pl.BlockSpec((TM, D_MODEL), lambda i: (jnp.clip(i - lo, 0, hi - lo - 1), 0))
    if entry:
        in_specs = [group_row(lo, hi) for lo, hi in tiles] + [_resident((TM, D_MODEL))]
    else:
        in_specs = [row]
    in_specs += [_resident((1, D_MODEL)), _resident((D_MODEL, 2 * D_FF)), _resident((D_FF, D_MODEL))]
    args = list(xs) + [gain.reshape(1, D_MODEL), w_in, w_out]
    if final:
        in_specs.append(_resident((1, D_MODEL)))
        args.append(final_gain.reshape(1, D_MODEL))
        out_specs = [group_row(lo, hi) for lo, hi in tiles]
        out_shape = [jax.ShapeDtypeStruct(((hi - lo) * TM, D_MODEL), F32) for lo, hi in tiles]
    else:
        out_specs = row
        out_shape = jax.ShapeDtypeStruct((n, D_MODEL), F32)
    return pl.pallas_call(
        functools.partial(_ffn_kernel, tiles=tiles, entry=entry, final=final),
        grid=(n // TM,),
        in_specs=in_specs,
        out_specs=out_specs,
        out_shape=out_shape,
        scratch_shapes=[pltpu.VMEM((TM, D_MODEL), F32)],
        compiler_params=_cparams(1),
        name="ffn",
    )(*args)


S_COLS = Q_LORA + KV_LORA + 2 * LANES


def _qkv_kernel(x_ref, g_ref, ws_ref, qn_ref, wqa_ref, wqb_ref, kvn_ref, wk_ref, wvt_ref, cos_ref, sin_ref,
                q_ref, k_ref, vt_ref):
    h = _rms(x_ref[...], g_ref[...]).astype(BF16)
    ps = _dot(h, ws_ref[...])
    cqn = _rms(ps[:, :Q_LORA], qn_ref[...]).astype(BF16)
    ckvn = _rms(ps[:, Q_LORA:Q_LORA + KV_LORA], kvn_ref[...]).astype(BF16)
    kr = ps[:, Q_LORA + KV_LORA:Q_LORA + KV_LORA + LANES]
    krr = ps[:, Q_LORA + KV_LORA + LANES:]
    cos = cos_ref[...]
    sin = sin_ref[...]
    scale = float((QK_NOPE + QK_ROPE) ** -0.5 * np.log2(np.e))
    lane = lax.broadcasted_iota(jnp.int32, (1, LANES), 1)
    cosq = jnp.where(lane < QK_NOPE, scale, cos * scale)
    sinq = sin * scale
    krope = kr * cos + krr * sin
    qa = _dot(cqn, wqa_ref[...])
    qb = _dot(cqn, wqb_ref[...])
    ka = _dot(ckvn, wk_ref[...])
    for hh in range(MLA_HEADS):
        sl = slice(hh * QK_PAD, (hh + 1) * QK_PAD)
        q_ref[hh] = (qa[:, sl] * cosq + qb[:, sl] * sinq).astype(BF16)
        k_ref[hh] = (ka[:, sl] + krope).astype(BF16)
    row_i = lax.broadcasted_iota(jnp.int32, (MLA_HEADS * V_ROWS, 1), 0)
    ones_row = jnp.where(row_i % V_ROWS == V_HEAD, 1.0, 0.0)
    vt_ref[...] = (_dot_nt(wvt_ref[...], ckvn) + ones_row).astype(BF16)


def _qkv(x, gain, ws, qn, wqa, wqb, kvn, wk, wvt, cos_t, sin_t):
    n = x.shape[0]
    hq = MLA_HEADS * QK_PAD
    row = lambda w: pl.BlockSpec((TM, w), lambda i: (i, 0))
    head = pl.BlockSpec((MLA_HEADS, TM, QK_PAD), lambda i: (0, i, 0))
    return pl.pallas_call(
        _qkv_kernel,
        grid=(n // TM,),
        in_specs=[row(D_MODEL), _resident((1, D_MODEL)), _resident((D_MODEL, S_COLS)),
                  _resident((1, Q_LORA)), _resident((Q_LORA, hq)), _resident((Q_LORA, hq)),
                  _resident((1, KV_LORA)), _resident((KV_LORA, hq)), _resident((MLA_HEADS * V_ROWS, KV_LORA)),
                  row(LANES), row(LANES)],
        out_specs=[head, head, pl.BlockSpec((MLA_HEADS * V_ROWS, TM), lambda i: (0, i))],
        out_shape=[jax.ShapeDtypeStruct((MLA_HEADS, n, QK_PAD), BF16),
                   jax.ShapeDtypeStruct((MLA_HEADS, n, QK_PAD), BF16),
                   jax.ShapeDtypeStruct((MLA_HEADS * V_ROWS, n), BF16)],
        compiler_params=_cparams(1),
        name="qkv_proj",
    )(x, gain.reshape(1, D_MODEL), ws, qn.reshape(1, Q_LORA), wqa, wqb, kvn.reshape(1, KV_LORA), wk, wvt,
      cos_t, sin_t)


DT_COLS = 2 * LANES


def _ssm_proj_kernel(tbl_ref, x_ref, xprev_ref, xnext_ref, g_ref, wz_ref, wx_ref, wdt_ref, dtb_ref, wg_ref,
                     cw_ref, cb_ref, dskip_ref, *rest, tm, meta):
    z_ref, xc_ref, skip_ref, dt_ref, gate_ref, ext_ref = rest[-6:]
    i = pl.program_id(0)
    hx = _rms(jnp.concatenate([x_ref[...], xprev_ref[...], xnext_ref[...]], axis=0), g_ref[...]).astype(BF16)
    h = hx[0:tm]
    if meta:
        row = lax.broadcasted_iota(jnp.int32, (tm, 1), 0)
        unused = row < PAD_ROWS
    xbc = _dot(hx, wx_ref[...])
    body = jnp.where(unused, 0.0, xbc[0:tm]) if meta else xbc[0:tm]
    ext_ref[0:HALO, :] = jnp.where(tbl_ref[2, i] != 0, xbc[tm:tm + HALO], 0.0)
    ext_ref[HALO:HALO + tm, :] = body
    ext_ref[HALO + tm:, :] = jnp.where(tbl_ref[3, i] != 0, xbc[tm + HALO:], 0.0)
    for t in range(CONV_CH // LANES):
        cols = slice(t * LANES, (t + 1) * LANES)
        conv = cb_ref[:, cols]
        for kk in range(D_CONV):
            lo = HALO - D_CONV // 2 + kk
            conv = conv + cw_ref[kk:kk + 1, cols] * ext_ref[lo:lo + tm, cols]
        xc = conv * _sigmoid(conv)
        xc_ref[:, cols] = xc.astype(BF16)
        if t < D_INNER // LANES:
            skip_ref[:, cols] = xc * dskip_ref[:, cols]
    z_ref[...] = _dot(h, wz_ref[...])
    gate_ref[...] = _sigmoid(_dot(h, wg_ref[...]))
    dtr = _dot(h, wdt_ref[...]) + dtb_ref[...]
    dt = jnp.maximum(dtr, 0.0) + jnp.log(1.0 + jnp.exp(-jnp.abs(dtr)))
    lane = lax.broadcasted_iota(jnp.int32, (1, DT_COLS), 1)
    dead = (lane % LANES) >= SSM_HEADS
    if meta:
        dead = jnp.logical_or(unused, dead)
    dt_ref[...] = jnp.where(dead, 0.0, dt)


def _ssm_proj(x, gain, weights, tbl, prev_outs, *, tm, first_blk, meta):
    n = x.shape[0]
    n_tiles = tbl.shape[1]
    row = lambda w: pl.BlockSpec((tm, w), lambda i, tbl: (first_blk + i, 0))
    widths = (D_INNER, CONV_CH, D_INNER, DT_COLS, 2 * D_MODEL)
    dtypes = (F32, BF16, F32, F32, F32)
    const = lambda shape: pl.BlockSpec(shape, lambda i, tbl: (0,) * len(shape), pipeline_mode=pl.Buffered(1))
    in_specs = [row(D_MODEL),
                pl.BlockSpec((HALO, D_MODEL), lambda i, tbl: (tbl[0, i], 0)),
                pl.BlockSpec((HALO, D_MODEL), lambda i, tbl: (tbl[1, i], 0)),
                const((1, D_MODEL)), const((D_MODEL, D_INNER)), const((D_MODEL, CONV_CH)),
                const((D_MODEL, DT_COLS)), const((1, DT_COLS)), const((D_MODEL, 2 * D_MODEL)),
                const((HALO, CONV_CH)), const((1, CONV_CH)), const((1, D_INNER))]
    args = [x, x, x, gain.reshape(1, D_MODEL)] + list(weights)
    aliases = {}
    if prev_outs is not None:
        for k, prev in enumerate(prev_outs):
            in_specs.append(pl.BlockSpec(memory_space=pl.ANY))
            args.append(prev)
            aliases[len(args)] = k
    return pl.pallas_call(
        functools.partial(_ssm_proj_kernel, tm=tm, meta=meta),
        grid_spec=pltpu.PrefetchScalarGridSpec(
            num_scalar_prefetch=1, grid=(n_tiles,), in_specs=in_specs,
            out_specs=[row(w) for w in widths],
            scratch_shapes=[pltpu.VMEM((tm + 2 * HALO, CONV_CH), F32)]),
        out_shape=[jax.ShapeDtypeStruct((n, w), d) for w, d in zip(widths, dtypes)],
        input_output_aliases=aliases,
        compiler_params=_cparams(1),
        name="ssm_proj_meta" if meta else "ssm_proj",
    )(tbl, *args)


def _halo_tables(groups, main_offs, seq_offs, n_main, n):
    main, meta = [], []
    for (b, s), mo, so in zip(groups, main_offs, seq_offs):
        for q in range(b):
            for t in range(s // TM_SSM):
                r0 = mo + q * s + t * TM_SSM
                prev = (n_main + (so + q + 1) * CHUNK) // HALO - 1 if t == 0 else r0 // HALO - 1
                last = t == s // TM_SSM - 1
                main.append((prev, r0 // HALO if last else (r0 + TM_SSM) // HALO, 1, 0 if last else 1))
            meta.append((0, (mo + q * s) // HALO, 0, 1))
    meta += [(0, 0, 0, 0)] * ((n - n_main) // CHUNK - len(meta))
    return (jnp.asarray(np.array(main, np.int32).T), jnp.asarray(np.array(meta, np.int32).T))


STRIP = 32
V_ROWS = 80


def _scores(hh, q_ref, k, s_ref, base, masked):
    n_keys = k.shape[0]
    s_t = _dot_nt(k, q_ref[hh])
    if masked:
        key_i = lax.broadcasted_iota(jnp.int32, (n_keys, 1), 0)
        s_t = jnp.where(key_i >= PAD_ROWS, s_t, NEG)
    s_ref[hh, pl.ds(base, n_keys), :] = s_t
    return jnp.max(s_t, axis=0, keepdims=True)


def _probs(hh, n_keys, m, cmax, s_ref, p_ref):
    tq = s_ref.shape[2]
    m_new = cmax if m is None else jnp.maximum(m, cmax)
    m_b = jnp.broadcast_to(m_new, (STRIP, tq))
    for r in range(n_keys // STRIP):
        rows = slice(r * STRIP, (r + 1) * STRIP)
        p_ref[hh, rows, :] = jnp.exp2(s_ref[hh, rows, :] - m_b).astype(BF16)
    return m_new, (None if m is None else jnp.exp2(m - m_new))


def _accumulate(hh, vts, alpha, p_ref, acc_ref):
    pv, lo = None, 0
    for vt in vts:
        part = _dot(vt, p_ref[hh, lo:lo + vt.shape[1], :])
        pv = part if pv is None else pv + part
        lo += vt.shape[1]
    acc_ref[hh] = pv if alpha is None else alpha * acc_ref[hh] + pv


def _attn_kernel(q_ref, km_ref, kmeta_ref, vtm_ref, vtmeta_ref, *rest, seq):
    o_ref, s0_ref, s1_ref, p0_ref, p1_ref, acc_ref = rest[-6:]
    tk = s1_ref.shape[1]
    n_chunks = seq // tk
    s_slots = (s0_ref, s1_ref)
    p_slots = (p0_ref, p1_ref)
    head_rows = lambda hh: slice(hh * V_ROWS, (hh + 1) * V_ROWS)

    carry = []
    for hh in range(2):
        cmax = jnp.maximum(_scores(hh, q_ref, kmeta_ref[hh], s0_ref, 0, True),
                           _scores(hh, q_ref, km_ref[hh, 0:tk, :], s0_ref, CHUNK, False))
        cmax_next = _scores(hh, q_ref, km_ref[hh, tk:2 * tk, :], s1_ref, 0, False)
        m, _ = _probs(hh, tk + CHUNK, None, cmax, s0_ref, p0_ref)
        _accumulate(hh, [vtmeta_ref[head_rows(hh), :], vtm_ref[head_rows(hh), 0:tk]], None, p0_ref, acc_ref)
        carry.append((m, cmax_next))

    def step(j, slot, carry, prefetch):
        off = _aligned(j * tk, tk)
        out = []
        for hh in range(2):
            m, cmax = carry[hh]
            if prefetch:
                cmax_next = _scores(hh, q_ref, km_ref[hh, pl.ds(_aligned(off + tk, tk), tk), :],
                                    s_slots[1 - slot], 0, False)
            else:
                cmax_next = cmax
            m, alpha = _probs(hh, tk, m, cmax, s_slots[slot], p_slots[slot])
            _accumulate(hh, [vtm_ref[head_rows(hh), pl.ds(off, tk)]], alpha, p_slots[slot], acc_ref)
            out.append((m, cmax_next))
        return tuple(out)

    def two_steps(i, carry):
        carry = step(2 * i + 1, 1, carry, True)
        return step(2 * i + 2, 0, carry, True)

    carry = lax.fori_loop(0, (n_chunks - 2) // 2, two_steps, tuple(carry))
    step(n_chunks - 1, 1, carry, False)
    o_t = jnp.concatenate([acc_ref[hh, 0:V_HEAD, :] / acc_ref[hh, V_HEAD:V_HEAD + 1, :] for hh in range(2)], axis=0)
    o_ref[...] = o_t.T.astype(BF16)


def _attn_tiles(seq):
    return (TQ, TK) if seq >= 4 * TK else (2 * TQ, TK // 2)


def _attention(q, k, vt, o_prev, *, n_main, main_off, seq_off, batch, seq, meta_queries):
    n = k.shape[1]
    meta_blk = n_main // CHUNK + seq_off
    kv_blk = main_off // seq
    if meta_queries:
        tq, tk = CHUNK, seq // 2
        grid = (batch, HEAD_PAIRS, 1)
        q_row = lambda s, p, i: meta_blk + s
    else:
        tq, tk = _attn_tiles(seq)
        grid = (batch, HEAD_PAIRS, seq // tq)
        q_row = lambda s, p, i: main_off // tq + s * (seq // tq) + i
    in_specs = [
        pl.BlockSpec((2, tq, QK_PAD), lambda s, p, i: (p, q_row(s, p, i), 0)),
        pl.BlockSpec((2, seq, QK_PAD), lambda s, p, i: (p, kv_blk + s, 0)),
        pl.BlockSpec((2, CHUNK, QK_PAD), lambda s, p, i: (p, meta_blk + s, 0)),
        pl.BlockSpec((2 * V_ROWS, seq), lambda s, p, i: (p, kv_blk + s)),
        pl.BlockSpec((2 * V_ROWS, CHUNK), lambda s, p, i: (p, meta_blk + s)),
    ]
    args = [q, k, k, vt, vt]
    aliases = {}
    if o_prev is not None:
        in_specs.append(pl.BlockSpec(memory_space=pl.ANY))
        args.append(o_prev)
        aliases = {len(args) - 1: 0}
    return pl.pallas_call(
        functools.partial(_attn_kernel, seq=seq),
        grid=grid,
        in_specs=in_specs,
        out_specs=pl.BlockSpec((tq, LANES), lambda s, p, i: (q_row(s, p, i), p)),
        out_shape=jax.ShapeDtypeStruct((n, ATT_W), BF16),
        scratch_shapes=[pltpu.VMEM((2, tk + CHUNK, tq), F32), pltpu.VMEM((2, tk, tq), F32),
                        pltpu.VMEM((2, tk + CHUNK, tq), BF16), pltpu.VMEM((2, tk, tq), BF16),
                        pltpu.VMEM((2, V_ROWS, tq), F32)],
        input_output_aliases=aliases,
        compiler_params=_cparams(3),
        name="attn_meta" if meta_queries else "attn_main",
    )(*args)


def _ssd_kernel(*refs, n_alias, backward):
    if backward:
        xc_ref, dt_ref, alog_ref, z_ref, yprev_ref, norm_ref = refs[:6]
        o_ref, state_ref = refs[6 + n_alias:]
    else:
        xc_ref, dt_ref, alog_ref, skip_ref = refs[:4]
        o_ref, state_ref = refs[4 + n_alias:]

    @pl.when(pl.program_id(1) == 0)
    def _():
        state_ref[...] = jnp.zeros_like(state_ref)

    dt = dt_ref[...]
    a = -jnp.exp(alog_ref[...]) * float(np.log2(np.e))
    da = dt * a
    hi = da.astype(BF16)
    r1 = da - hi.astype(F32)
    mid = r1.astype(BF16)
    lo3 = (r1 - mid.astype(F32)).astype(BF16)
    row_i = lax.broadcasted_iota(jnp.int32, (CHUNK, CHUNK), 0)
    col_i = lax.broadcasted_iota(jnp.int32, (CHUNK, CHUNK), 1)
    keep = (col_i >= row_i) if backward else (col_i <= row_i)
    tri = jnp.where(keep, 1.0, 0.0).astype(BF16)
    acs = _dot(tri, hi) + _dot(tri, mid) + _dot(tri, lo3)
    acs_t = acs.T
    dt_t = dt.T
    edge = 0 if backward else CHUNK - 1
    a_end_t = acs_t[:, edge:edge + 1]
    w_t = jnp.exp2(a_end_t - acs_t) * dt_t
    lane = lax.broadcasted_iota(jnp.int32, (1, LANES), 1)
    first_half = lane < SSM_HEADDIM
    zero = jnp.zeros((), BF16)

    c_off = D_INNER + SSM_GROUPS * D_STATE
    y_all = []
    for g in range(SSM_GROUPS):
        bgb = xc_ref[:, D_INNER + g * D_STATE:D_INNER + (g + 1) * D_STATE]
        cgb = xc_ref[:, c_off + g * D_STATE:c_off + (g + 1) * D_STATE]
        bg_t = bgb.astype(F32).T
        cb = _dot_nt(cgb, bgb)
        gl = slice(g * 4 * LANES, (g + 1) * 4 * LANES)
        prev = state_ref[:, gl]
        y_off = _dot(cgb, prev.astype(BF16))
        new_tiles, decay_tiles, y_tiles = [], [], []
        for qq in range(4):
            h1 = g * 8 + qq * 2
            h2 = h1 + 1
            pair = slice(g * 4 * LANES + qq * LANES, g * 4 * LANES + (qq + 1) * LANES)
            x_pair = xc_ref[:, pair]
            x_lo = jnp.where(first_half, x_pair, zero)
            x_hi = jnp.where(first_half, zero, x_pair)
            x_diag = jnp.concatenate([x_lo, x_hi], axis=0)
            ws, bs, cols = [], [], []
            for hh in (h1, h2):
                col = jnp.broadcast_to(acs[:, hh:hh + 1], (CHUNK, CHUNK))
                seg = col - acs_t[hh:hh + 1, :]
                lmat = jnp.exp2(jnp.where(keep, seg, NEG))
                ws.append((cb * lmat * dt_t[hh:hh + 1, :]).astype(BF16))
                bs.append((bg_t * w_t[hh:hh + 1, :]).astype(BF16))
                cols.append(col)
            y_diag = _dot(jnp.concatenate(ws, axis=1), x_diag)
            new_tiles.append(_dot(jnp.concatenate(bs, axis=1), x_diag))
            acs_pair = jnp.where(first_half, cols[0], cols[1])
            decay_tiles.append(jnp.exp2(acs_pair[edge:edge + 1, :]))
            y_tiles.append(y_diag + y_off[:, qq * LANES:(qq + 1) * LANES] * jnp.exp2(acs_pair))
        state_ref[:, gl] = prev * jnp.concatenate(decay_tiles, axis=1) + jnp.concatenate(new_tiles, axis=1)
        y_all.append(jnp.concatenate(y_tiles, axis=1))
    if not backward:
        o_ref[...] = jnp.concatenate(y_all, axis=1) + skip_ref[...]
    else:
        z = z_ref[...]
        y = (jnp.concatenate(y_all, axis=1) + yprev_ref[...]) * (z * _sigmoid(z))
        gw = D_INNER // SSM_GROUPS
        outs = []
        for g in range(SSM_GROUPS):
            yg = y[:, g * gw:(g + 1) * gw]
            outs.append(yg * lax.rsqrt(jnp.mean(yg * yg, axis=-1, keepdims=True) + EPS))
        o_ref[...] = (jnp.concatenate(outs, axis=1) * norm_ref[...]).astype(BF16)


def _ssd_pass(xc, dt, alog, extra, o_prev, *, n_main, main_off, seq_off, batch, seq, backward):
    n = xc.shape[0]
    n_chunks = seq // CHUNK + 1
    meta_blk = n_main // CHUNK + seq_off
    main_blk = main_off // CHUNK

    def blk(s, step):
        c = (n_chunks - 1 - step) if backward else step
        return jnp.where(c == 0, meta_blk + s, main_blk + s * (seq // CHUNK) + c - 1)

    d = 1 if backward else 0
    chunk_rows = lambda w: pl.BlockSpec((CHUNK, w), lambda s, t: (blk(s, t), 0))
    dt_spec = pl.BlockSpec((CHUNK, LANES), lambda s, t: (blk(s, t), d))
    if backward:
        z, y_fwd, norm = extra
        in_specs = [chunk_rows(CONV_CH), dt_spec, _resident((1, LANES)),
                    chunk_rows(D_INNER), chunk_rows(D_INNER), _resident((1, D_INNER))]
        args = [xc, dt, alog, z, y_fwd, norm]
    else:
        (skip,) = extra
        in_specs = [chunk_rows(CONV_CH), dt_spec, _resident((1, LANES)), chunk_rows(D_INNER)]
        args = [xc, dt, alog, skip]
    aliases = {}
    if o_prev is not None:
        in_specs.append(pl.BlockSpec(memory_space=pl.ANY))
        args.append(o_prev)
        aliases = {len(args) - 1: 0}
    return pl.pallas_call(
        functools.partial(_ssd_kernel, n_alias=len(aliases), backward=backward),
        grid=(batch, n_chunks),
        in_specs=in_specs,
        out_specs=chunk_rows(D_INNER),
        out_shape=jax.ShapeDtypeStruct((n, D_INNER), BF16 if backward else F32),
        scratch_shapes=[pltpu.VMEM((D_STATE, D_INNER), F32)],
        input_output_aliases=aliases,
        compiler_params=_cparams(2),
        name="ssd_bwd" if backward else "ssd_fwd",
    )(*args)


def _merge_kernel(oa_ref, om_ref, gate_ref, x_ref, wba_ref, wbm_ref, wo_ref, o_ref):
    y_a = _dot(oa_ref[...], wba_ref[...])
    y_m = _dot(om_ref[...], wbm_ref[...])
    mix = gate_ref[:, :D_MODEL] * y_a + gate_ref[:, D_MODEL:] * y_m
    o_ref[...] = x_ref[...] + _dot(mix.astype(BF16), wo_ref[...])


def _merge(o_a, o_m, gates, x, wba, wbm, wo):
    n = x.shape[0]
    row = lambda w: pl.BlockSpec((TM, w), lambda i: (i, 0))
    return pl.pallas_call(
        _merge_kernel,
        grid=(n // TM,),
        in_specs=[row(ATT_W), row(D_INNER), row(2 * D_MODEL), row(D_MODEL),
                  _resident((ATT_W, D_MODEL)), _resident((D_INNER, D_MODEL)), _resident((D_MODEL, D_MODEL))],
        out_specs=row(D_MODEL),
        out_shape=jax.ShapeDtypeStruct((n, D_MODEL), F32),
        compiler_params=_cparams(1),
        name="merge_out",
    )(o_a, o_m, gates, x, wba, wbm, wo)


def _head_tiles(w, width):
    k = w.shape[0]
    w = w.reshape(k, MLA_HEADS, width)
    return jnp.pad(w, ((0, 0), (0, 0), (0, QK_PAD - width))).reshape(k, MLA_HEADS * QK_PAD)


def _rot_cols(w):
    half = QK_ROPE // 2
    return jnp.concatenate([-w[..., half:], w[..., :half]], axis=-1)


def _layer_weights(l, w_in, w_uq, w_ukv, conv_w, conv_b, a_log, dt_bias, d_skip, w_branch):
    offs = np.cumsum((Q_LORA, KV_LORA, QK_ROPE, D_INNER, CONV_CH, 2 * SSM_HEADS, 2 * D_MODEL))
    wi = w_in[l]
    w_cq, w_ckv, w_kr = wi[:, :offs[0]], wi[:, offs[0]:offs[1]], wi[:, offs[1]:offs[2]]
    w_z, w_xbc = wi[:, offs[2]:offs[3]], wi[:, offs[3]:offs[4]]
    w_dt, w_gate = wi[:, offs[4]:offs[5]], wi[:, offs[5]:offs[6]]
    rope_tile = lambda w: jnp.pad(w, ((0, 0), (QK_NOPE, QK_PAD - QK_NOPE - QK_ROPE)))
    ws = jnp.concatenate([w_cq, w_ckv, rope_tile(w_kr), rope_tile(_rot_cols(w_kr))], axis=1).astype(BF16)
    uq = w_uq[l].reshape(Q_LORA, MLA_HEADS, QK_NOPE + QK_ROPE)
    uq_rot = jnp.concatenate([jnp.zeros_like(uq[..., :QK_NOPE]), _rot_cols(uq[..., QK_NOPE:])], axis=-1)
    wqa = _head_tiles(uq.reshape(Q_LORA, -1), QK_NOPE + QK_ROPE).astype(BF16)
    wqb = _head_tiles(uq_rot.reshape(Q_LORA, -1), QK_NOPE + QK_ROPE).astype(BF16)
    ukv = w_ukv[l].reshape(KV_LORA, MLA_HEADS, QK_NOPE + V_HEAD)
    wk = _head_tiles(ukv[..., :QK_NOPE].reshape(KV_LORA, -1), QK_NOPE).astype(BF16)
    wvt = jnp.pad(jnp.transpose(ukv[..., QK_NOPE:], (1, 2, 0)), ((0, 0), (0, V_ROWS - V_HEAD), (0, 0)))
    wvt = wvt.reshape(MLA_HEADS * V_ROWS, KV_LORA).astype(BF16)
    dt_tile = lambda w: jnp.pad(w, ((0, 0), (0, LANES - SSM_HEADS)))
    wdt = jnp.concatenate([dt_tile(w_dt[:, :SSM_HEADS]), dt_tile(w_dt[:, SSM_HEADS:])], axis=1).astype(BF16)
    dtb = jnp.concatenate([dt_tile(dt_bias[l][0:1]), dt_tile(dt_bias[l][1:2])], axis=1)
    return dict(
        ws=ws, wqa=wqa, wqb=wqb, wk=wk, wvt=wvt,
        wz=w_z.astype(BF16), wx=w_xbc.astype(BF16), wdt=wdt, dtb=dtb, wg=w_gate.astype(BF16),
        cw=jnp.pad(conv_w[l], ((0, HALO - D_CONV), (0, 0))), cb=conv_b[l].reshape(1, CONV_CH),
        alog=[dt_tile(a_log[l][0:1]), dt_tile(a_log[l][1:2])],
        dskip=jnp.repeat(d_skip[l], SSM_HEADDIM).reshape(1, D_INNER),
        wba=w_branch[l][:ATT_W].astype(BF16), wbm=w_branch[l][ATT_W:].astype(BF16),
    )


def _rope_tables(groups, n_main, n):
    inv = ROPE_THETA ** (-jnp.arange(0, QK_ROPE, 2, dtype=F32) / QK_ROPE)
    pos = [jnp.tile(jnp.arange(N_META, N_META + s, dtype=F32), b) for b, s in groups]
    meta = jnp.arange(CHUNK, dtype=F32) - PAD_ROWS
    pos.append(jnp.tile(meta, (n - n_main) // CHUNK))
    ang = jnp.concatenate(pos)[:, None] * inv[None, :]
    ang = jnp.concatenate([ang, ang], axis=-1)
    place = lambda t: jnp.pad(t, ((0, 0), (QK_NOPE, QK_PAD - QK_NOPE - QK_ROPE)))
    return place(jnp.cos(ang)), place(jnp.sin(ang))


def kernel(x_prompt, x_sample, meta_tokens, ffn1_norm, ffn1_w_in, ffn1_w_out, mix_norm, w_in, q_norm, w_uq, kv_norm, w_ukv, conv_w, conv_b, a_log, dt_bias, d_skip, ssm_norm, w_branch, w_out, ffn2_norm, ffn2_w_in, ffn2_w_out, final_norm):
    xs_in = (x_prompt, x_sample)
    groups = [(x.shape[0], x.shape[1]) for x in xs_in]
    depth = w_in.shape[0]
    n_seq = sum(b for b, _ in groups)
    n_main = sum(b * s for b, s in groups)
    n = n_main + -(-(n_seq * CHUNK) // TM) * TM
    main_offs, seq_offs = [], []
    mo = so = 0
    for b, s in groups:
        tq, tk = _attn_tiles(s)
        assert s % tq == 0 and s % (2 * tk) == 0 and s % TM_SSM == 0 and mo % s == 0, (groups,)
        main_offs.append(mo)
        seq_offs.append(so)
        mo += b * s
        so += b

    meta_block = jnp.concatenate([jnp.zeros((PAD_ROWS, D_MODEL), F32), meta_tokens.astype(F32)], axis=0)
    meta_tile = jnp.tile(meta_block, (TM // CHUNK, 1))
    tiles = [(mo // TM, (mo + b * s) // TM) for (b, s), mo in zip(groups, main_offs)]
    x = [xi.reshape(-1, D_MODEL).astype(F32) for xi in xs_in] + [meta_tile]
    cos_t, sin_t = _rope_tables(groups, n_main, n)
    tbl_main, tbl_meta = _halo_tables(groups, main_offs, seq_offs, n_main, n)

    for l in range(depth):
        lw = _layer_weights(l, w_in, w_uq, w_ukv, conv_w, conv_b, a_log, dt_bias, d_skip, w_branch)
        x = _ffn(x if l == 0 else [x], n, tiles, ffn1_norm[l], ffn1_w_in[l].astype(BF16),
                 ffn1_w_out[l].astype(BF16), entry=l == 0)

        q, k, vt = _qkv(x, mix_norm[l], lw["ws"], q_norm[l], lw["wqa"], lw["wqb"], kv_norm[l], lw["wk"], lw["wvt"],
                       cos_t, sin_t)
        ssm_w = (lw["wz"], lw["wx"], lw["wdt"], lw["dtb"], lw["wg"], lw["cw"], lw["cb"], lw["dskip"])
        proj = _ssm_proj(x, mix_norm[l], ssm_w, tbl_main, None, tm=TM_SSM, first_blk=0, meta=False)
        z, xc, skip, dt, gates = _ssm_proj(x, mix_norm[l], ssm_w, tbl_meta, proj, tm=CHUNK,
                                           first_blk=n_main // CHUNK, meta=True)

        o_a = y_f = o_m = None
        geo = [dict(n_main=n_main, main_off=main_offs[g], seq_off=seq_offs[g], batch=groups[g][0], seq=groups[g][1])
               for g in range(len(groups))]
        for gg in geo:
            o_a = _attention(q, k, vt, o_a, meta_queries=False, **gg)
            o_a = _attention(q, k, vt, o_a, meta_queries=True, **gg)
            y_f = _ssd_pass(xc, dt, lw["alog"][0], (skip,), y_f, backward=False, **gg)
        for gg in geo:
            o_m = _ssd_pass(xc, dt, lw["alog"][1], (z, y_f, ssm_norm[l].reshape(1, D_INNER)), o_m,
                            backward=True, **gg)

        x = _merge(o_a, o_m, gates, x, lw["wba"], lw["wbm"], w_out[l].astype(BF16))
        x = _ffn([x], n, tiles, ffn2_norm[l], ffn2_w_in[l].astype(BF16), ffn2_w_out[l].astype(BF16),
                 final_gain=final_norm if l == depth - 1 else None)

    return tuple(y.reshape(b, s, D_MODEL).astype(xi.dtype) for y, (b, s), xi in zip(x, groups, xs_in))
```
